```python
import jax, jax.numpy as jnp
from jax import lax
import numpy as np

D_MODEL = 2048
BATCH = 8
SEQ = 4096
DEPTH = 2

N_MIXERS = 2
HG_HEADS = 16
HG_KEY_DIM = 128
HG_VAL_DIM = D_MODEL // HG_HEADS
HG_WIDTH = HG_HEADS * HG_KEY_DIM
HG_V_WIDTH = HG_HEADS * HG_VAL_DIM
HG_CHUNK = 64
SG_WIDTH = D_MODEL
SG_GROUPS = 16
SG_GROUP_DIM = SG_WIDTH // SG_GROUPS
SG_CHUNK = 128
D_FF = 5632
CONV_WIDTH = 3
ALPHA = (2 * DEPTH) ** 0.25
BETA = (8 * DEPTH) ** -0.25
LN_EPS = 1e-5
RMS_EPS = 1e-6
N_HG_LAYERS = (DEPTH + 1) // 2
N_SG_LAYERS = DEPTH // 2

kernel_name = 'hgrn2_gmlp_convffn_deepnorm_hybrid'


def layer_norm(x, g, b):
    xf = x.astype(jnp.float32)
    mu = jnp.mean(xf, axis=-1, keepdims=True)
    xc = xf - mu
    var = jnp.mean(xc * xc, axis=-1, keepdims=True)
    y = xc * lax.rsqrt(var + LN_EPS) * g.astype(jnp.float32) + b.astype(jnp.float32)
    return y.astype(x.dtype)


def hgrn2_mixer(x, w_in, norm_g, w_out, lb):
    b_, s_, _ = x.shape
    n = s_ // HG_CHUNK
    proj = x @ w_in
    q, f, i, g = jnp.split(proj, [HG_WIDTH, 2 * HG_WIDTH, 2 * HG_WIDTH + HG_V_WIDTH], axis=-1)
    f = f.astype(jnp.float32)
    q = jax.nn.silu(q.astype(jnp.float32))
    v = i.astype(jnp.float32)
    log_forget = jnp.logaddexp(jnp.log(lb), jnp.log1p(-lb) + jax.nn.log_sigmoid(f))
    k = (1.0 - lb) * jax.nn.sigmoid(-f)

    def to_chunks(t, d):
        return t.reshape(b_, n, HG_CHUNK, HG_HEADS, d).transpose(1, 0, 3, 2, 4)

    qc = to_chunks(q, HG_KEY_DIM)
    kc = to_chunks(k, HG_KEY_DIM)
    lfc = to_chunks(log_forget, HG_KEY_DIM)
    vc = to_chunks(v, HG_VAL_DIM)
    mask = jnp.tril(jnp.ones((HG_CHUNK, HG_CHUNK), dtype=bool))

    def step(state, inp):
        q_c, k_c, v_c, lf_c = inp
        cum = jnp.cumsum(lf_c, axis=2)
        rel = cum[:, :, :, None, :] - cum[:, :, None, :, :]
        decay = jnp.exp(jnp.where(mask[:, :, None], rel, -jnp.inf))
        scores = jnp.einsum('bhtk,bhsk,bhtsk->bhts', q_c, k_c, decay)
        out = (jnp.einsum('bhts,bhsv->bhtv', scores, v_c)
               + jnp.einsum('bhtk,bhkv->bhtv', q_c * jnp.exp(cum), state))
        last = cum[:, :, -1:, :]
        new_state = (jnp.exp(last[:, :, 0, :, None]) * state
                     + jnp.einsum('bhsk,bhsv->bhkv', k_c * jnp.exp(last - cum), v_c))
        return new_state, out

    state0 = jnp.zeros((b_, HG_HEADS, HG_KEY_DIM, HG_VAL_DIM), jnp.float32)
    _, o = lax.scan(step, state0, (qc, kc, vc, lfc))
    o = o.transpose(1, 0, 3, 2, 4).reshape(b_, s_, HG_HEADS, HG_VAL_DIM)
    o = o * lax.rsqrt(jnp.mean(o * o, axis=-1, keepdims=True) + RMS_EPS)
    o = o * norm_g.astype(jnp.float32).reshape(HG_HEADS, HG_VAL_DIM)
    o = o.reshape(b_, s_, HG_V_WIDTH) * jax.nn.silu(g.astype(jnp.float32))
    return o.astype(x.dtype) @ w_out


def chunked_gmlp(x, w_in, ln_g, ln_b, w_s, b_s, w_out):
    b_, s_, _ = x.shape
    n = s_ // SG_CHUNK
    z = jax.nn.gelu(x @ w_in, approximate=False)
    u, v = jnp.split(z, 2, axis=-1)
    v = layer_norm(v, ln_g, ln_b).reshape(b_, n, SG_CHUNK, SG_GROUPS, SG_GROUP_DIM)
    w_causal = w_s * jnp.tril(jnp.ones((SG_CHUNK, SG_CHUNK), w_s.dtype))
    gate = jnp.einsum('gts,bnsgc->bntgc', w_causal, v) + b_s.T[:, :, None]
    y = u * gate.reshape(b_, s_, SG_WIDTH)
    return y @ w_out


def conv_ffn(x, w_up, conv_w, conv_b, w_down):
    s_ = x.shape[1]
    h = x @ w_up
    a, b = jnp.split(h, 2, axis=-1)
    a_pad = jnp.pad(a, ((0, 0), (CONV_WIDTH - 1, 0), (0, 0)))
    a = sum(conv_w[j] * a_pad[:, j:j + s_] for j in range(CONV_WIDTH)) + conv_b
    return (jax.nn.silu(a) * b) @ w_down


def setup_inputs(seed: int = 0) -> dict:
    key = jax.random.key(seed)
    ks = jax.random.split(key, 24)
    nrm = jax.random.normal
    f32 = jnp.float32
    d = D_MODEL
    hg_in_cols = 2 * HG_WIDTH + 2 * HG_V_WIDTH
    return {
        'x': nrm(ks[0], (BATCH, SEQ, d), f32),
        'lb_logits': 0.5 * nrm(ks[1], (DEPTH + 1, HG_WIDTH), f32),
        'hg_w_in': nrm(ks[2], (N_HG_LAYERS, d, hg_in_cols), f32) * d ** -0.5,
        'hg_norm_g': 1.0 + 0.02 * nrm(ks[3], (N_HG_LAYERS, HG_V_WIDTH), f32),
        'hg_w_out': nrm(ks[4], (N_HG_LAYERS, HG_V_WIDTH, d), f32) * HG_V_WIDTH ** -0.5 * BETA,
        'sg_w_in': nrm(ks[5], (N_SG_LAYERS, d, 2 * SG_WIDTH), f32) * d ** -0.5,
        'sg_ln_g': 1.0 + 0.02 * nrm(ks[6], (N_SG_LAYERS, SG_WIDTH), f32),
        'sg_ln_b': 0.02 * nrm(ks[7], (N_SG_LAYERS, SG_WIDTH), f32),
        'sg_w_s': nrm(ks[8], (N_SG_LAYERS, SG_GROUPS, SG_CHUNK, SG_CHUNK), f32) * 0.5 * SG_CHUNK ** -0.5,
        'sg_b_s': 1.0 + 0.1 * nrm(ks[9], (N_SG_LAYERS, SG_GROUPS, SG_CHUNK), f32),
        'sg_w_out': nrm(ks[10], (N_SG_LAYERS, SG_WIDTH, d), f32) * SG_WIDTH ** -0.5 * BETA,
        'ffn_w_up': nrm(ks[11], (DEPTH, d, 2 * D_FF), f32) * d ** -0.5,
        'ffn_conv_w': nrm(ks[12], (DEPTH, CONV_WIDTH, D_FF), f32) * CONV_WIDTH ** -0.5,
        'ffn_conv_b': 0.02 * nrm(ks[13], (DEPTH, D_FF), f32),
        'ffn_w_down': nrm(ks[14], (DEPTH, D_FF, d), f32) * D_FF ** -0.5 * BETA,
        'ln1_g': 1.0 + 0.02 * nrm(ks[15], (DEPTH, d), f32),
        'ln1_b': 0.02 * nrm(ks[16], (DEPTH, d), f32),
        'ln2_g': 1.0 + 0.02 * nrm(ks[17], (DEPTH, d), f32),
        'ln2_b': 0.02 * nrm(ks[18], (DEPTH, d), f32),
    }


def reference(x, lb_logits, hg_w_in, hg_norm_g, hg_w_out, sg_w_in, sg_ln_g, sg_ln_b,
              sg_w_s, sg_b_s, sg_w_out, ffn_w_up, ffn_conv_w, ffn_conv_b, ffn_w_down,
              ln1_g, ln1_b, ln2_g, ln2_b):
    lower_bounds = jnp.cumsum(jax.nn.softmax(lb_logits.astype(jnp.float32), axis=0), axis=0)
    h = x
    for layer in range(DEPTH):
        occ = layer // N_MIXERS
        if layer % N_MIXERS == 0:
            mixed = hgrn2_mixer(h, hg_w_in[occ], hg_norm_g[occ], hg_w_out[occ], lower_bounds[layer])
        else:
            mixed = chunked_gmlp(h, sg_w_in[occ], sg_ln_g[occ], sg_ln_b[occ],
                                 sg_w_s[occ], sg_b_s[occ], sg_w_out[occ])
        h = layer_norm(ALPHA * h + mixed, ln1_g[layer], ln1_b[layer])
        ffn = conv_ffn(h, ffn_w_up[layer], ffn_conv_w[layer], ffn_conv_b[layer], ffn_w_down[layer])
        h = layer_norm(ALPHA * h + ffn, ln2_g[layer], ln2_b[layer])
    return h
```

```python
import functools

import numpy as np
import jax
import jax.numpy as jnp
from jax import lax
from jax.experimental import pallas as pl
from jax.experimental.pallas import tpu as pltpu

D_MODEL = 2048
DEPTH = 2
N_MIXERS = 2
HG_HEADS = 16
HG_DIM = 128
HG_CHUNK = 64
SG_GROUPS = 16
SG_DIM = 128
SG_CHUNK = 128
D_FF = 5632
CONV_WIDTH = 3
ALPHA = (2 * DEPTH) ** 0.25
LN_EPS = 1e-5
RMS_EPS = 1e-6

F32 = jnp.float32
BF16 = jnp.bfloat16

V7X_VMEM_LIMIT_BYTES = 56 * 1024 * 1024
BF16_SUBLANE_TILE = 16


def _tiles(batch, seq):
    rows = batch * seq
    t = 512
    assert seq % t == 0 and rows % t == 0 and D_FF % 512 == 0
    return dict(hg_rows=t, out_rows=t, ffn_rows=t, ffn_cols=512, sg_rows=t, ln_rows=32)


def _sigmoid_pair(z):
    e = jnp.exp(-jnp.abs(z))
    r = 1.0 / (1.0 + e)
    er = e * r
    pos = z >= 0
    return jnp.where(pos, r, er), jnp.where(pos, er, r)


def _silu(z):
    return z * _sigmoid_pair(z)[0]


def _gelu_exact(z):
    return 0.5 * z * (1.0 + lax.erf(z * (2.0 ** -0.5)))


def _layer_norm(z, g, b):
    mu = jnp.mean(z, axis=-1, keepdims=True)
    zc = z - mu
    var = jnp.mean(zc * zc, axis=-1, keepdims=True)
    return zc * lax.rsqrt(var + LN_EPS) * g + b


def _dot(a, b):
    return jnp.dot(a, b, preferred_element_type=F32)


def _dot_nt(a, b):
    return lax.dot_general(a, b, (((1,), (1,)), ((), ())), preferred_element_type=F32)


def _dot_tn(a, b):
    return lax.dot_general(a, b, (((0,), (0,)), ((), ())), preferred_element_type=F32)


def _params(*semantics):
    return pltpu.CompilerParams(dimension_semantics=semantics,
                                vmem_limit_bytes=V7X_VMEM_LIMIT_BYTES)


_HG_LEVELS = (32, 16, 8, 4, 2, 1)
_N_LEV = len(_HG_LEVELS)


def _hg_tables():
    c = HG_CHUNK
    e = np.zeros((8 * c, c), np.float32)
    masks = np.zeros((_N_LEV + 1, c, c), np.float32)
    t = np.arange(c)
    for li, h in enumerate(_HG_LEVELS):
        for tt in range(c):
            pos = tt % (2 * h)
            m = tt - pos + h
            if pos >= h:
                e[li * c + tt, m:tt + 1] = 1.0
            else:
                e[li * c + tt, tt + 1:m] = 1.0
        blk = t // (2 * h)
        upper = (t % (2 * h)) >= h
        masks[li] = (blk[:, None] == blk[None, :]) & upper[:, None] & (~upper)[None, :]
    masks[_N_LEV] = np.eye(c)
    e[6 * c:7 * c] = np.tril(np.ones((c, c)))
    e[7 * c:8 * c] = np.triu(np.ones((c, c)), 1)
    return e, masks


def _hgrn2_kernel(layer, rows, x_ref, w_ref, lbl_ref, ng_ref, e_ref, m_ref, o_ref,
                  xb_ref, proj_ref, st_ref):
    s = pl.program_id(1)
    h = pl.program_id(2)
    c = HG_CHUNK
    d = HG_DIM

    @pl.when(h == 0)
    def _():
        xb_ref[...] = x_ref[...].astype(BF16)

    @pl.when(s == 0)
    def _():
        st_ref[h] = jnp.zeros((d, d), F32)

    proj_ref[...] = _dot(xb_ref[...], w_ref[...])

    lrows = [lbl_ref[i:i + 1, :] for i in range(DEPTH + 1)]
    mx = functools.reduce(jnp.maximum, lrows)
    ex = [jnp.exp(r - mx) for r in lrows]
    lb = sum(ex[:layer + 1]) / sum(ex)
    one_m_lb = 1.0 - lb
    ng = ng_ref[...]
    e_all = e_ref[...]

    def chunk(ci, carry):
        r0 = pl.multiple_of(ci * c, c)
        blk = proj_ref[pl.ds(r0, c), :]
        q = _silu(blk[:, 0:d])
        sig, sig_neg = _sigmoid_pair(blk[:, d:2 * d])
        lf = jnp.log(lb + one_m_lb * sig)
        k = one_m_lb * sig_neg
        v_b = blk[:, 2 * d:3 * d].astype(BF16)
        gate = _silu(blk[:, 3 * d:4 * d])

        hi = lf.astype(BF16)
        r1 = lf - hi.astype(F32)
        mid = r1.astype(BF16)
        lo = (r1 - mid.astype(F32)).astype(BF16)
        x3 = _dot(e_all, jnp.concatenate([hi, mid, lo], axis=1))
        xs = x3[:, 0:d] + x3[:, d:2 * d] + x3[:, 2 * d:3 * d]

        scores = _dot_nt(q.astype(BF16), k.astype(BF16)) * m_ref[_N_LEV]
        for li in range(_N_LEV):
            w = jnp.exp(xs[li * c:(li + 1) * c])
            scores = scores + _dot_nt((q * w).astype(BF16), (k * w).astype(BF16)) * m_ref[li]

        cum = xs[6 * c:7 * c]
        rev = xs[7 * c:8 * c]
        last = cum[c - 1:c, :]
        st = st_ref[h]
        out = (_dot(scores.astype(BF16), v_b)
               + _dot_nt((q * jnp.exp(cum)).astype(BF16), st.astype(BF16)))
        st_ref[h] = st * jnp.exp(last) + _dot_tn(v_b, (k * jnp.exp(rev)).astype(BF16))

        out = out * lax.rsqrt(jnp.mean(out * out, axis=-1, keepdims=True) + RMS_EPS)
        o_ref[pl.ds(r0, c), :] = (out * ng * gate).astype(BF16)
        return carry

    lax.fori_loop(0, rows // c, chunk, 0)


def _hgrn2_mixer(x2d, batch, seq, layer, lb_logits, w_in, norm_g, rows):
    m = x2d.shape[0]
    nst = seq // rows
    d = HG_DIM
    w = w_in.reshape(D_MODEL, 4, HG_HEADS, d).transpose(2, 0, 1, 3)
    w = w.reshape(HG_HEADS, D_MODEL, 4 * d).astype(BF16)
    lbl = lb_logits.astype(F32).reshape(DEPTH + 1, HG_HEADS, d).transpose(1, 0, 2)
    ng = norm_g.astype(F32).reshape(HG_HEADS, 1, d)
    e_np, m_np = _hg_tables()
    e_all = jnp.asarray(e_np, BF16)
    masks = jnp.asarray(m_np, F32)
    return pl.pallas_call(
        functools.partial(_hgrn2_kernel, layer, rows),
        grid=(batch, nst, HG_HEADS),
        in_specs=[
            pl.BlockSpec((rows, D_MODEL), lambda b, s, h: (b * nst + s, 0)),
            pl.BlockSpec((None, D_MODEL, 4 * d), lambda b, s, h: (h, 0, 0)),
            pl.BlockSpec((None, DEPTH + 1, d), lambda b, s, h: (h, 0, 0)),
            pl.BlockSpec((None, 1, d), lambda b, s, h: (h, 0, 0)),
            pl.BlockSpec(e_all.shape, lambda b, s, h: (0, 0)),
            pl.BlockSpec(masks.shape, lambda b, s, h: (0, 0, 0)),
        ],
        out_specs=pl.BlockSpec((rows, d), lambda b, s, h: (b * nst + s, h)),
        out_shape=jax.ShapeDtypeStruct((m, HG_HEADS * d), BF16),
        scratch_shapes=[
            pltpu.VMEM((rows, D_MODEL), BF16),
            pltpu.VMEM((rows, 4 * d), F32),
            pltpu.VMEM((HG_HEADS, d, d), F32),
        ],
        compiler_params=_params("arbitrary", "arbitrary", "arbitrary"),
        name="hgrn2_mixer",
    )(x2d, w, lbl, ng, e_all, masks)


def _ln_rows(ln_rows, n_rows, z_of, g, b, o_ref):
    def body(r, carry):
        rs = pl.ds(pl.multiple_of(r * ln_rows, ln_rows), ln_rows)
        o_ref[rs, :] = _layer_norm(z_of(rs), g, b)
        return carry

    lax.fori_loop(0, n_rows // ln_rows, body, 0)


def _outproj_ln_kernel(rows, ln_rows, y_ref, w_ref, x_ref, g_ref, b_ref, o_ref, acc_ref):
    acc_ref[...] = _dot(y_ref[...], w_ref[...])
    _ln_rows(ln_rows, rows, lambda rs: ALPHA * x_ref[rs, :] + acc_ref[rs, :],
             g_ref[...], b_ref[...], o_ref)


def _outproj_ln(y2d, w_out, x2d, g, b, rows, ln_rows):
    m = x2d.shape[0]
    return pl.pallas_call(
        functools.partial(_outproj_ln_kernel, rows, ln_rows),
        grid=(m // rows,),
        in_specs=[
            pl.BlockSpec((rows, D_MODEL), lambda i: (i, 0)),
            pl.BlockSpec((D_MODEL, D_MODEL), lambda i: (0, 0)),
            pl.BlockSpec((rows, D_MODEL), lambda i: (i, 0)),
            pl.BlockSpec((1, D_MODEL), lambda i: (0, 0)),
            pl.BlockSpec((1, D_MODEL), lambda i: (0, 0)),
        ],
        out_specs=pl.BlockSpec((rows, D_MODEL), lambda i: (i, 0)),
        out_shape=jax.ShapeDtypeStruct((m, D_MODEL), F32),
        scratch_shapes=[pltpu.VMEM((rows, D_MODEL), F32)],
        compiler_params=_params("arbitrary"),
        name="outproj_ln",
    )(y2d, w_out.astype(BF16), x2d, g.astype(F32).reshape(1, D_MODEL),
      b.astype(F32).reshape(1, D_MODEL))


_HALO = BF16_SUBLANE_TILE
_CONV_ROWS = 64


def _ffn_kernel(rows, ln_rows, seq_tiles, x_ref, xp_ref, wa_ref, wb_ref, cw_ref, cb_ref,
                wd_ref, g_ref, b_ref, o_ref, xe_ref, a_ref, bb_ref, hm_ref, acc_ref):
    i = pl.program_id(0)
    j = pl.program_id(1)

    @pl.when(j == 0)
    def _():
        keep = (i % seq_tiles) != 0
        xe_ref[0:_HALO, :] = jnp.where(keep, xp_ref[...], 0.0).astype(BF16)
        xe_ref[_HALO:, :] = x_ref[...].astype(BF16)

    a_ref[...] = _dot(xe_ref[...], wa_ref[...])
    bb_ref[...] = _dot(xe_ref[_HALO:, :], wb_ref[...])
    cw = cw_ref[...]
    cb = cb_ref[...]
    pad = 8

    def blk(r, carry):
        r0 = pl.multiple_of(r * _CONV_ROWS, _CONV_ROWS)
        a = a_ref[pl.ds(r0 + _HALO - pad, _CONV_ROWS + pad), :]
        a_1 = pltpu.roll(a, 1, 0)[pad:]
        a_2 = pltpu.roll(a, 2, 0)[pad:]
        conv = cw[0:1] * a_2 + cw[1:2] * a_1 + cw[2:3] * a[pad:] + cb
        hm = _silu(conv) * bb_ref[pl.ds(r0, _CONV_ROWS), :]
        hm_ref[pl.ds(r0, _CONV_ROWS), :] = hm.astype(BF16)
        return carry

    lax.fori_loop(0, rows // _CONV_ROWS, blk, 0)
    contrib = _dot(hm_ref[...], wd_ref[...])

    @pl.when(j == 0)
    def _():
        acc_ref[...] = contrib

    @pl.when(j > 0)
    def _():
        acc_ref[...] += contrib

    @pl.when(j == pl.num_programs(1) - 1)
    def _():
        _ln_rows(ln_rows, rows, lambda rs: ALPHA * x_ref[rs, :] + acc_ref[rs, :],
                 g_ref[...], b_ref[...], o_ref)


def _conv_ffn_ln(x2d, seq, w_up, conv_w, conv_b, w_down, g, b, rows, cols, ln_rows):
    m = x2d.shape[0]
    nj = D_FF // cols
    halo_blocks = rows // _HALO
    w_up = w_up.astype(BF16)
    return pl.pallas_call(
        functools.partial(_ffn_kernel, rows, ln_rows, seq // rows),
        grid=(m // rows, nj),
        in_specs=[
            pl.BlockSpec((rows, D_MODEL), lambda i, j: (i, 0)),
            pl.BlockSpec((_HALO, D_MODEL), lambda i, j: (jnp.maximum(i * halo_blocks - 1, 0), 0)),
            pl.BlockSpec((D_MODEL, cols), lambda i, j: (0, j)),
            pl.BlockSpec((D_MODEL, cols), lambda i, j: (0, nj + j)),
            pl.BlockSpec((CONV_WIDTH, cols), lambda i, j: (0, j)),
            pl.BlockSpec((1, cols), lambda i, j: (0, j)),
            pl.BlockSpec((cols, D_MODEL), lambda i, j: (j, 0)),
            pl.BlockSpec((1, D_MODEL), lambda i, j: (0, 0)),
            pl.BlockSpec((1, D_MODEL), lambda i, j: (0, 0)),
        ],
        out_specs=pl.BlockSpec((rows, D_MODEL), lambda i, j: (i, 0)),
        out_shape=jax.ShapeDtypeStruct((m, D_MODEL), F32),
        scratch_shapes=[
            pltpu.VMEM((_HALO + rows, D_MODEL), BF16),
            pltpu.VMEM((_HALO + rows, cols), F32),
            pltpu.VMEM((rows, cols), F32),
            pltpu.VMEM((rows, cols), BF16),
            pltpu.VMEM((rows, D_MODEL), F32),
        ],
        compiler_params=_params("arbitrary", "arbitrary"),
        name="conv_ffn_ln",
    )(x2d, x2d, w_up, w_up, conv_w.astype(F32), conv_b.astype(F32).reshape(1, D_FF),
      w_down.astype(BF16), g.astype(F32).reshape(1, D_MODEL), b.astype(F32).reshape(1, D_MODEL))


def _gmlp_kernel(rows, ln_rows, x_ref, w_ref, lng_ref, lnb_ref, ws_ref, bias_ref, o_ref,
                 xb_ref, z_ref, vn_ref, gate_ref, wc_ref):
    j = pl.program_id(1)
    c = SG_CHUNK
    d = SG_DIM

    @pl.when(j == 0)
    def _():
        xb_ref[...] = x_ref[...].astype(BF16)
        causal = (lax.broadcasted_iota(jnp.int32, (c, c), 0)
                  >= lax.broadcasted_iota(jnp.int32, (c, c), 1))
        for g in range(SG_GROUPS):
            wc_ref[g] = jnp.where(causal, ws_ref[g], 0.0).astype(BF16)

    z_ref[...] = _dot(xb_ref[...], w_ref[...])

    @pl.when(j == 0)
    def _():
        lng = lng_ref[...]
        lnb = lnb_ref[...]

        def norm(r, carry):
            rs = pl.ds(pl.multiple_of(r * ln_rows, ln_rows), ln_rows)
            vn_ref[rs, :] = _layer_norm(_gelu_exact(z_ref[rs, :]), lng, lnb).astype(BF16)
            return carry

        lax.fori_loop(0, rows // ln_rows, norm, 0)

        def mix(ci, carry):
            rs = pl.ds(pl.multiple_of(ci * c, c), c)
            for g in range(SG_GROUPS):
                gs = slice(g * d, (g + 1) * d)
                gate_ref[rs, gs] = _dot(wc_ref[g], vn_ref[rs, gs]) + bias_ref[:, gs]
            return carry

        lax.fori_loop(0, rows // c, mix, 0)

    @pl.when(j == 1)
    def _():
        def gate(r, carry):
            rs = pl.ds(pl.multiple_of(r * ln_rows, ln_rows), ln_rows)
            o_ref[rs, :] = (_gelu_exact(z_ref[rs, :]) * gate_ref[rs, :]).astype(BF16)
            return carry

        lax.fori_loop(0, rows // ln_rows, gate, 0)


def _gmlp_mixer(x2d, w_in, ln_g, ln_b, w_s, b_s, rows, ln_rows):
    m = x2d.shape[0]
    width = SG_GROUPS * SG_DIM
    bias = jnp.broadcast_to(b_s.astype(F32).T[:, :, None], (SG_CHUNK, SG_GROUPS, SG_DIM))
    bias = bias.reshape(SG_CHUNK, width)
    return pl.pallas_call(
        functools.partial(_gmlp_kernel, rows, ln_rows),
        grid=(m // rows, 2),
        in_specs=[
            pl.BlockSpec((rows, D_MODEL), lambda i, j: (i, 0)),
            pl.BlockSpec((D_MODEL, width), lambda i, j: (0, 1 - j)),
            pl.BlockSpec((1, width), lambda i, j: (0, 0)),
            pl.BlockSpec((1, width), lambda i, j: (0, 0)),
            pl.BlockSpec((SG_GROUPS, SG_CHUNK, SG_CHUNK), lambda i, j: (0, 0, 0)),
            pl.BlockSpec((SG_CHUNK, width), lambda i, j: (0, 0)),
        ],
        out_specs=pl.BlockSpec((rows, width), lambda i, j: (i, 0)),
        out_shape=jax.ShapeDtypeStruct((m, width), BF16),
        scratch_shapes=[
            pltpu.VMEM((rows, D_MODEL), BF16),
            pltpu.VMEM((rows, width), F32),
            pltpu.VMEM((rows, width), BF16),
            pltpu.VMEM((rows, width), F32),
            pltpu.VMEM((SG_GROUPS, SG_CHUNK, SG_CHUNK), BF16),
        ],
        compiler_params=_params("arbitrary", "arbitrary"),
        name="gmlp_mixer",
    )(x2d, w_in.astype(BF16), ln_g.astype(F32).reshape(1, width),
      ln_b.astype(F32).reshape(1, width), w_s.astype(F32), bias)


def kernel(x, lb_logits, hg_w_in, hg_norm_g, hg_w_out, sg_w_in, sg_ln_g, sg_ln_b, sg_w_s,
           sg_b_s, sg_w_out, ffn_w_up, ffn_conv_w, ffn_conv_b, ffn_w_down, ln1_g, ln1_b,
           ln2_g, ln2_b):
    batch, seq, _ = x.shape
    t = _tiles(batch, seq)
    h = x.astype(F32).reshape(batch * seq, D_MODEL)
    for layer in range(DEPTH):
        occ = layer // N_MIXERS
        if layer % N_MIXERS == 0:
            mixed = _hgrn2_mixer(h, batch, seq, layer, lb_logits, hg_w_in[occ], hg_norm_g[occ],
                                 t["hg_rows"])
            w_out = hg_w_out[occ]
        else:
            mixed = _gmlp_mixer(h, sg_w_in[occ], sg_ln_g[occ], sg_ln_b[occ], sg_w_s[occ],
                                sg_b_s[occ], t["sg_rows"], t["ln_rows"])
            w_out = sg_w_out[occ]
        h = _outproj_ln(mixed, w_out, h, ln1_g[layer], ln1_b[layer], t["out_rows"], t["ln_rows"])
        h = _conv_ffn_ln(h, seq, ffn_w_up[layer], ffn_conv_w[layer], ffn_conv_b[layer],
                         ffn_w_down[layer], ln2_g[layer], ln2_b[layer],
                         t["ffn_rows"], t["ffn_cols"], t["ln_rows"])
    return h.reshape(batch, seq, D_MODEL).astype(x.dtype)
```

```python
import functools

import numpy as np
import jax
import jax.numpy as jnp
from jax import lax
from jax.experimental import pallas as pl
from jax.experimental.pallas import tpu as pltpu

D_MODEL = 2048
DEPTH = 2
N_MIXERS = 2
HG_HEADS = 16
HG_DIM = 128
HG_CHUNK = 64
SG_GROUPS = 16
SG_DIM = 128
SG_CHUNK = 128
D_FF = 5632
CONV_WIDTH = 3
ALPHA = (2 * DEPTH) ** 0.25
LN_EPS = 1e-5
RMS_EPS = 1e-6

F32 = jnp.float32
BF16 = jnp.bfloat16

V7X_VMEM_LIMIT_BYTES = 56 * 1024 * 1024
BF16_SUBLANE_TILE = 16


def _tiles(batch, seq):
    rows = batch * seq
    t = 512
    assert seq % t == 0 and rows % t == 0 and D_FF % 512 == 0
    return dict(hg_rows=t, out_rows=t, ffn_rows=t, ffn_cols=512, sg_rows=t, ln_rows=32)


def _sigmoid_pair(z):
    e = jnp.exp(-jnp.abs(z))
    r = 1.0 / (1.0 + e)
    er = e * r
    pos = z >= 0
    return jnp.where(pos, r, er), jnp.where(pos, er, r)


def _silu(z):
    return z / (1.0 + jnp.exp(-z))


def _gelu_exact(z):
    return 0.5 * z * (1.0 + lax.erf(z * (2.0 ** -0.5)))


def _layer_norm(z, g, b):
    mu = jnp.mean(z, axis=-1, keepdims=True)
    zc = z - mu
    var = jnp.mean(zc * zc, axis=-1, keepdims=True)
    return zc * lax.rsqrt(var + LN_EPS) * g + b


def _dot(a, b):
    return jnp.dot(a, b, preferred_element_type=F32)


def _dot_nt(a, b):
    return lax.dot_general(a, b, (((1,), (1,)), ((), ())), preferred_element_type=F32)


def _dot_tn(a, b):
    return lax.dot_general(a, b, (((0,), (0,)), ((), ())), preferred_element_type=F32)


def _params(*semantics):
    return pltpu.CompilerParams(dimension_semantics=semantics,
                                vmem_limit_bytes=V7X_VMEM_LIMIT_BYTES)


_HG_LEVELS = (32, 16, 8, 4, 2, 1)
_N_LEV = len(_HG_LEVELS)


def _hg_tables():
    c = HG_CHUNK
    e = np.zeros((8 * c, c), np.float32)
    masks = np.zeros((_N_LEV + 1, c, c), np.float32)
    t = np.arange(c)
    for li, h in enumerate(_HG_LEVELS):
        for tt in range(c):
            pos = tt % (2 * h)
            m = tt - pos + h
            if pos >= h:
                e[li * c + tt, m:tt + 1] = 1.0
            else:
                e[li * c + tt, tt + 1:m] = 1.0
        blk = t // (2 * h)
        upper = (t % (2 * h)) >= h
        masks[li] = (blk[:, None] == blk[None, :]) & upper[:, None] & (~upper)[None, :]
    masks[_N_LEV] = np.eye(c)
    e[6 * c:7 * c] = np.tril(np.ones((c, c)))
    e[7 * c:8 * c] = np.triu(np.ones((c, c)), 1)
    return e, masks


def _hgrn2_kernel(layer, rows, x_ref, w_ref, lbl_ref, ng_ref, e_ref, m_ref, o_ref,
                  xb_ref, proj_ref, st_ref):
    s = pl.program_id(1)
    h = pl.program_id(2)
    c = HG_CHUNK
    d = HG_DIM

    @pl.when(h == 0)
    def _():
        xb_ref[...] = x_ref[...].astype(BF16)

    @pl.when(s == 0)
    def _():
        st_ref[h] = jnp.zeros((d, d), F32)

    proj_ref[...] = _dot(xb_ref[...], w_ref[...])

    lrows = [lbl_ref[i:i + 1, :] for i in range(DEPTH + 1)]
    mx = functools.reduce(jnp.maximum, lrows)
    ex = [jnp.exp(r - mx) for r in lrows]
    lb = sum(ex[:layer + 1]) / sum(ex)
    one_m_lb = 1.0 - lb
    ng = ng_ref[...]
    e3 = e_ref[...]

    def gates(r0):
        blk = proj_ref[r0:r0 + c, :]
        q = _silu(blk[:, 0:d])
        sig, sig_neg = _sigmoid_pair(blk[:, d:2 * d])
        lf = jnp.log(lb + one_m_lb * sig)
        k = one_m_lb * sig_neg
        return q, k, lf, blk[:, 2 * d:3 * d].astype(BF16), _silu(blk[:, 3 * d:4 * d])

    st = st_ref[h]
    for pair in range(rows // (2 * c)):
        r0 = pair * 2 * c
        ga = gates(r0)
        gb = gates(r0 + c)
        lf2 = jnp.concatenate([ga[2], gb[2]], axis=1)
        hi = lf2.astype(BF16)
        r1 = lf2 - hi.astype(F32)
        mid = r1.astype(BF16)
        lo = (r1 - mid.astype(F32)).astype(BF16)
        xs2 = _dot(e3, jnp.concatenate([hi, mid, lo], axis=0))

        for half, (q, k, _, v_b, gate) in enumerate((ga, gb)):
            xs = xs2[:, half * d:(half + 1) * d]
            scores = _dot_nt(q.astype(BF16), k.astype(BF16)) * m_ref[_N_LEV]
            for li in range(_N_LEV):
                w = jnp.exp(xs[li * c:(li + 1) * c])
                scores = scores + _dot_nt((q * w).astype(BF16), (k * w).astype(BF16)) * m_ref[li]
            cum = xs[6 * c:7 * c]
            rev = xs[7 * c:8 * c]
            last = cum[c - 1:c, :]
            kv = _dot_tn(v_b, (k * jnp.exp(rev)).astype(BF16))
            out = (_dot(scores.astype(BF16), v_b)
                   + _dot_nt((q * jnp.exp(cum)).astype(BF16), st.astype(BF16)))
            st = st * jnp.exp(last) + kv
            out = out * lax.rsqrt(jnp.mean(out * out, axis=-1, keepdims=True) + RMS_EPS)
            o_ref[r0 + half * c:r0 + (half + 1) * c, :] = (out * ng * gate).astype(BF16)
    st_ref[h] = st


def _hgrn2_mixer(x2d, batch, seq, layer, lb_logits, w_in, norm_g, rows):
    m = x2d.shape[0]
    nst = seq // rows
    d = HG_DIM
    w = w_in.reshape(D_MODEL, 4, HG_HEADS, d).transpose(2, 0, 1, 3)
    w = w.reshape(HG_HEADS, D_MODEL, 4 * d).astype(BF16)
    lbl = lb_logits.astype(F32).reshape(DEPTH + 1, HG_HEADS, d).transpose(1, 0, 2)
    ng = norm_g.astype(F32).reshape(HG_HEADS, 1, d)
    e_np, m_np = _hg_tables()
    e_all = jnp.asarray(np.concatenate([e_np, e_np, e_np], axis=1), BF16)
    masks = jnp.asarray(m_np, F32)
    return pl.pallas_call(
        functools.partial(_hgrn2_kernel, layer, rows),
        grid=(batch, nst, HG_HEADS),
        in_specs=[
            pl.BlockSpec((rows, D_MODEL), lambda b, s, h: (b * nst + s, 0)),
            pl.BlockSpec((None, D_MODEL, 4 * d), lambda b, s, h: (h, 0, 0)),
            pl.BlockSpec((None, DEPTH + 1, d), lambda b, s, h: (h, 0, 0)),
            pl.BlockSpec((None, 1, d), lambda b, s, h: (h, 0, 0)),
            pl.BlockSpec(e_all.shape, lambda b, s, h: (0, 0)),
            pl.BlockSpec(masks.shape, lambda b, s, h: (0, 0, 0)),
        ],
        out_specs=pl.BlockSpec((rows, d), lambda b, s, h: (b * nst + s, h)),
        out_shape=jax.ShapeDtypeStruct((m, HG_HEADS * d), BF16),
        scratch_shapes=[
            pltpu.VMEM((rows, D_MODEL), BF16),
            pltpu.VMEM((rows, 4 * d), F32),
            pltpu.VMEM((HG_HEADS, d, d), F32),
        ],
        compiler_params=_params("arbitrary", "arbitrary", "arbitrary"),
        name="hgrn2_mixer",
    )(x2d, w, lbl, ng, e_all, masks)


_LN_UNROLL = 4


def _ln_rows(ln_rows, n_rows, z_of, g, b, o_ref):
    def body(r, carry):
        rs = pl.ds(pl.multiple_of(r * ln_rows, ln_rows), ln_rows)
        o_ref[rs, :] = _layer_norm(z_of(rs), g, b)
        return carry

    lax.fori_loop(0, n_rows // ln_rows, body, 0, unroll=_LN_UNROLL)


def _outproj_ln_kernel(rows, ln_rows, y_ref, w_ref, x_ref, g_ref, b_ref, o_ref, acc_ref):
    acc_ref[...] = _dot(y_ref[...], w_ref[...])
    _ln_rows(ln_rows, rows, lambda rs: ALPHA * x_ref[rs, :] + acc_ref[rs, :],
             g_ref[...], b_ref[...], o_ref)


def _outproj_ln(y2d, w_out, x2d, g, b, rows, ln_rows):
    m = x2d.shape[0]
    return pl.pallas_call(
        functools.partial(_outproj_ln_kernel, rows, ln_rows),
        grid=(m // rows,),
        in_specs=[
            pl.BlockSpec((rows, D_MODEL), lambda i: (i, 0)),
            pl.BlockSpec((D_MODEL, D_MODEL), lambda i: (0, 0)),
            pl.BlockSpec((rows, D_MODEL), lambda i: (i, 0)),
            pl.BlockSpec((1, D_MODEL), lambda i: (0, 0)),
            pl.BlockSpec((1, D_MODEL), lambda i: (0, 0)),
        ],
        out_specs=pl.BlockSpec((rows, D_MODEL), lambda i: (i, 0)),
        out_shape=jax.ShapeDtypeStruct((m, D_MODEL), F32),
        scratch_shapes=[pltpu.VMEM((rows, D_MODEL), F32)],
        compiler_params=_params("arbitrary"),
        name="outproj_ln",
    )(y2d, w_out.astype(BF16), x2d, g.astype(F32).reshape(1, D_MODEL),
      b.astype(F32).reshape(1, D_MODEL))


_HALO = BF16_SUBLANE_TILE
_CONV_ROWS = 64


def _ffn_kernel(rows, ln_rows, seq_tiles, x_ref, xp_ref, wa_ref, wb_ref, cw_ref, cb_ref,
                wd_ref, g_ref, b_ref, o_ref, xe_ref, a_ref, bb_ref, hm_ref, acc_ref):
    i = pl.program_id(0)
    j = pl.program_id(1)

    @pl.when(j == 0)
    def _():
        keep = (i % seq_tiles) != 0
        xe_ref[0:_HALO, :] = jnp.where(keep, xp_ref[...], 0.0).astype(BF16)
        xe_ref[_HALO:, :] = x_ref[...].astype(BF16)
        acc_ref[...] = jnp.zeros_like(acc_ref)

    a_ref[...] = _dot(xe_ref[...], wa_ref[...])
    bb_ref[...] = _dot(xe_ref[_HALO:, :], wb_ref[...])
    cw = cw_ref[...]
    cb = cb_ref[...]
    for r0 in range(0, rows, _CONV_ROWS):
        a_0 = a_ref[_HALO + r0:_HALO + r0 + _CONV_ROWS, :]
        a_1 = a_ref[_HALO + r0 - 1:_HALO + r0 - 1 + _CONV_ROWS, :]
        a_2 = a_ref[_HALO + r0 - 2:_HALO + r0 - 2 + _CONV_ROWS, :]
        conv = cw[0:1] * a_2 + cw[1:2] * a_1 + cw[2:3] * a_0 + cb
        hm = _silu(conv) * bb_ref[r0:r0 + _CONV_ROWS, :]
        hm_ref[r0:r0 + _CONV_ROWS, :] = hm.astype(BF16)
    acc_ref[...] += _dot(hm_ref[...], wd_ref[...])

    @pl.when(j == pl.num_programs(1) - 1)
    def _():
        _ln_rows(ln_rows, rows, lambda rs: ALPHA * x_ref[rs, :] + acc_ref[rs, :],
                 g_ref[...], b_ref[...], o_ref)


def _conv_ffn_ln(x2d, seq, w_up, conv_w, conv_b, w_down, g, b, rows, cols, ln_rows):
    m = x2d.shape[0]
    nj = D_FF // cols
    halo_blocks = rows // _HALO
    w_up = w_up.astype(BF16)
    return pl.pallas_call(
        functools.partial(_ffn_kernel, rows, ln_rows, seq // rows),
        grid=(m // rows, nj),
        in_specs=[
            pl.BlockSpec((rows, D_MODEL), lambda i, j: (i, 0)),
            pl.BlockSpec((_HALO, D_MODEL), lambda i, j: (jnp.maximum(i * halo_blocks - 1, 0), 0)),
            pl.BlockSpec((D_MODEL, cols), lambda i, j: (0, j)),
            pl.BlockSpec((D_MODEL, cols), lambda i, j: (0, nj + j)),
            pl.BlockSpec((CONV_WIDTH, cols), lambda i, j: (0, j)),
            pl.BlockSpec((1, cols), lambda i, j: (0, j)),
            pl.BlockSpec((cols, D_MODEL), lambda i, j: (j, 0)),
            pl.BlockSpec((1, D_MODEL), lambda i, j: (0, 0)),
            pl.BlockSpec((1, D_MODEL), lambda i, j: (0, 0)),
        ],
        out_specs=pl.BlockSpec((rows, D_MODEL), lambda i, j: (i, 0)),
        out_shape=jax.ShapeDtypeStruct((m, D_MODEL), F32),
        scratch_shapes=[
            pltpu.VMEM((_HALO + rows, D_MODEL), BF16),
            pltpu.VMEM((_HALO + rows, cols), F32),
            pltpu.VMEM((rows, cols), F32),
            pltpu.VMEM((rows, cols), BF16),
            pltpu.VMEM((rows, D_MODEL), F32),
        ],
        compiler_params=_params("arbitrary", "arbitrary"),
        name="conv_ffn_ln",
    )(x2d, x2d, w_up, w_up, conv_w.astype(F32), conv_b.astype(F32).reshape(1, D_FF),
      w_down.astype(BF16), g.astype(F32).reshape(1, D_MODEL), b.astype(F32).reshape(1, D_MODEL))


def _gmlp_kernel(rows, ln_rows, x_ref, w_ref, lng_ref, lnb_ref, ws_ref, bias_ref, o_ref,
                 xb_ref, z_ref, vn_ref, gate_ref, wc_ref):
    j = pl.program_id(1)
    c = SG_CHUNK
    d = SG_DIM

    @pl.when(j == 0)
    def _():
        xb_ref[...] = x_ref[...].astype(BF16)
        causal = (lax.broadcasted_iota(jnp.int32, (c, c), 0)
                  >= lax.broadcasted_iota(jnp.int32, (c, c), 1))
        for g in range(SG_GROUPS):
            wc_ref[g] = jnp.where(causal, ws_ref[g], 0.0).astype(BF16)

    z_ref[...] = _dot(xb_ref[...], w_ref[...])

    @pl.when(j == 0)
    def _():
        lng = lng_ref[...]
        lnb = lnb_ref[...]

        def norm(r, carry):
            rs = pl.ds(pl.multiple_of(r * ln_rows, ln_rows), ln_rows)
            vn_ref[rs, :] = _layer_norm(_gelu_exact(z_ref[rs, :]), lng, lnb).astype(BF16)
            return carry

        lax.fori_loop(0, rows // ln_rows, norm, 0)

        def mix(ci, carry):
            rs = pl.ds(pl.multiple_of(ci * c, c), c)
            for g in range(SG_GROUPS):
                gs = slice(g * d, (g + 1) * d)
                gate_ref[rs, gs] = _dot(wc_ref[g], vn_ref[rs, gs]) + bias_ref[:, gs]
            return carry

        lax.fori_loop(0, rows // c, mix, 0)

    @pl.when(j == 1)
    def _():
        def gate(r, carry):
            rs = pl.ds(pl.multiple_of(r * ln_rows, ln_rows), ln_rows)
            o_ref[rs, :] = (_gelu_exact(z_ref[rs, :]) * gate_ref[rs, :]).astype(BF16)
            return carry

        lax.fori_loop(0, rows // ln_rows, gate, 0)


def _gmlp_mixer(x2d, w_in, ln_g, ln_b, w_s, b_s, rows, ln_rows):
    m = x2d.shape[0]
    width = SG_GROUPS * SG_DIM
    bias = jnp.broadcast_to(b_s.astype(F32).T[:, :, None], (SG_CHUNK, SG_GROUPS, SG_DIM))
    bias = bias.reshape(SG_CHUNK, width)
    return pl.pallas_call(
        functools.partial(_gmlp_kernel, rows, ln_rows),
        grid=(m // rows, 2),
        in_specs=[
            pl.BlockSpec((rows, D_MODEL), lambda i, j: (i, 0)),
            pl.BlockSpec((D_MODEL, width), lambda i, j: (0, 1 - j)),
            pl.BlockSpec((1, width), lambda i, j: (0, 0)),
            pl.BlockSpec((1, width), lambda i, j: (0, 0)),
            pl.BlockSpec((SG_GROUPS, SG_CHUNK, SG_CHUNK), lambda i, j: (0, 0, 0)),
            pl.BlockSpec((SG_CHUNK, width), lambda i, j: (0, 0)),
        ],
        out_specs=pl.BlockSpec((rows, width), lambda i, j: (i, 0)),
        out_shape=jax.ShapeDtypeStruct((m, width), BF16),
        scratch_shapes=[
            pltpu.VMEM((rows, D_MODEL), BF16),
            pltpu.VMEM((rows, width), F32),
            pltpu.VMEM((rows, width), BF16),
            pltpu.VMEM((rows, width), F32),
            pltpu.VMEM((SG_GROUPS, SG_CHUNK, SG_CHUNK), BF16),
        ],
        compiler_params=_params("arbitrary", "arbitrary"),
        name="gmlp_mixer",
    )(x2d, w_in.astype(BF16), ln_g.astype(F32).reshape(1, width),
      ln_b.astype(F32).reshape(1, width), w_s.astype(F32), bias)


def kernel(x, lb_logits, hg_w_in, hg_norm_g, hg_w_out, sg_w_in, sg_ln_g, sg_ln_b, sg_w_s,
           sg_b_s, sg_w_out, ffn_w_up, ffn_conv_w, ffn_conv_b, ffn_w_down, ln1_g, ln1_b,
           ln2_g, ln2_b):
    batch, seq, _ = x.shape
    t = _tiles(batch, seq)
    h = x.astype(F32).reshape(batch * seq, D_MODEL)
    for layer in range(DEPTH):
        occ = layer // N_MIXERS
        if layer % N_MIXERS == 0:
            mixed = _hgrn2_mixer(h, batch, seq, layer, lb_logits, hg_w_in[occ], hg_norm_g[occ],
                                 t["hg_rows"])
            w_out = hg_w_out[occ]
        else:
            mixed = _gmlp_mixer(h, sg_w_in[occ], sg_ln_g[occ], sg_ln_b[occ], sg_w_s[occ],
                                sg_b_s[occ], t["sg_rows"], t["ln_rows"])
            w_out = sg_w_out[occ]
        h = _outproj_ln(mixed, w_out, h, ln1_g[layer], ln1_b[layer], t["out_rows"], t["ln_rows"])
        h = _conv_ffn_ln(h, seq, ffn_w_up[layer], ffn_conv_w[layer], ffn_conv_b[layer],
                         ffn_w_down[layer], ln2_g[layer], ln2_b[layer],
                         t["ffn_rows"], t["ffn_cols"], t["ln_rows"])
    return h.reshape(batch, seq, D_MODEL).astype(x.dtype)
```

```python
import functools

import numpy as np
import jax
import jax.numpy as jnp
from jax import lax
from jax.experimental import pallas as pl
from jax.experimental.pallas import tpu as pltpu

D_MODEL = 2048
DEPTH = 2
N_MIXERS = 2
HG_HEADS = 16
HG_DIM = 128
HG_CHUNK = 64
SG_GROUPS = 16
SG_DIM = 128
SG_CHUNK = 128
D_FF = 5632
CONV_WIDTH = 3
ALPHA = (2 * DEPTH) ** 0.25
LN_EPS = 1e-5
RMS_EPS = 1e-6

F32 = jnp.float32
BF16 = jnp.bfloat16

V7X_VMEM_LIMIT_BYTES = 56 * 1024 * 1024
BF16_SUBLANE_TILE = 16


def _tiles(batch, seq):
    rows = batch * seq
    t = 512
    assert seq % t == 0 and rows % t == 0 and D_FF % 512 == 0
    return dict(hg_rows=t, out_rows=t, ffn_rows=t, ffn_cols=512, sg_rows=t, ln_rows=32)


def _sigmoid_pair(z):
    e = jnp.exp(-jnp.abs(z))
    r = 1.0 / (1.0 + e)
    er = e * r
    pos = z >= 0
    return jnp.where(pos, r, er), jnp.where(pos, er, r)


def _silu(z):
    return z / (1.0 + jnp.exp(-z))


def _gelu_exact(z):
    return 0.5 * z * (1.0 + lax.erf(z * (2.0 ** -0.5)))


def _layer_norm(z, g, b):
    mu = jnp.mean(z, axis=-1, keepdims=True)
    zc = z - mu
    var = jnp.mean(zc * zc, axis=-1, keepdims=True)
    return zc * lax.rsqrt(var + LN_EPS) * g + b


def _dot(a, b):
    return jnp.dot(a, b, preferred_element_type=F32)


def _dot_nt(a, b):
    return lax.dot_general(a, b, (((1,), (1,)), ((), ())), preferred_element_type=F32)


def _dot_tn(a, b):
    return lax.dot_general(a, b, (((0,), (0,)), ((), ())), preferred_element_type=F32)


def _params(*semantics):
    return pltpu.CompilerParams(dimension_semantics=semantics,
                                vmem_limit_bytes=V7X_VMEM_LIMIT_BYTES)


_HG_LEVELS = (32, 16, 8, 4, 2, 1)
_N_LEV = len(_HG_LEVELS)


def _hg_tables():
    c = HG_CHUNK
    e = np.zeros((8 * c, c), np.float32)
    masks = np.zeros((_N_LEV + 1, c, c), np.float32)
    t = np.arange(c)
    for li, h in enumerate(_HG_LEVELS):
        for tt in range(c):
            pos = tt % (2 * h)
            m = tt - pos + h
            if pos >= h:
                e[li * c + tt, m:tt + 1] = 1.0
            else:
                e[li * c + tt, tt + 1:m] = 1.0
        blk = t // (2 * h)
        upper = (t % (2 * h)) >= h
        masks[li] = (blk[:, None] == blk[None, :]) & upper[:, None] & (~upper)[None, :]
    masks[_N_LEV] = np.eye(c)
    e[6 * c:7 * c] = np.tril(np.ones((c, c)))
    e[7 * c:8 * c] = np.triu(np.ones((c, c)), 1)
    return e, masks


def _hg_head(layer, rows, proj_ref, lbl, ng, e3, m_ref, st, o_ref, lane0, q_ref, k_ref,
             xs_ref, fill):
    c = HG_CHUNK
    d = HG_DIM
    n_chunks = rows // c
    lrows = [lbl[i:i + 1, :] for i in range(DEPTH + 1)]
    mx = functools.reduce(jnp.maximum, lrows)
    ex = [jnp.exp(r - mx) for r in lrows]
    lb = sum(ex[:layer + 1]) / sum(ex)
    one_m_lb = 1.0 - lb

    def rs(ci):
        return slice(ci * c, (ci + 1) * c)

    def xs(ci, block):
        return xs_ref[ci // 2, block * c:(block + 1) * c, (ci % 2) * d:(ci % 2 + 1) * d]

    fill[0]()
    for pair in range(n_chunks // 2):
        lfs = []
        for ci in (2 * pair, 2 * pair + 1):
            q_ref[rs(ci), :] = _silu(proj_ref[rs(ci), 0:d])
            sig, sig_neg = _sigmoid_pair(proj_ref[rs(ci), d:2 * d])
            lfs.append(jnp.log(lb + one_m_lb * sig))
            k_ref[rs(ci), :] = one_m_lb * sig_neg
        lf2 = jnp.concatenate(lfs, axis=1)
        hi = lf2.astype(BF16)
        r1 = lf2 - hi.astype(F32)
        mid = r1.astype(BF16)
        lo = (r1 - mid.astype(F32)).astype(BF16)
        xs_ref[pair] = _dot(e3, jnp.concatenate([hi, mid, lo], axis=0))
    fill[1]()
    scores = []
    for ci in range(n_chunks):
        q = q_ref[rs(ci), :]
        k = k_ref[rs(ci), :]
        sc = _dot_nt(q.astype(BF16), k.astype(BF16)) * m_ref[_N_LEV]
        for li in range(_N_LEV):
            w = jnp.exp(xs(ci, li))
            sc = sc + _dot_nt((q * w).astype(BF16), (k * w).astype(BF16)) * m_ref[li]
        scores.append(sc.astype(BF16))
    fill[2]()
    intra, kvs, qhs, decays = [], [], [], []
    for ci in range(n_chunks):
        v_b = proj_ref[rs(ci), 2 * d:3 * d].astype(BF16)
        cum = xs(ci, 6)
        kvs.append(_dot_tn(v_b, (k_ref[rs(ci), :] * jnp.exp(xs(ci, 7))).astype(BF16)))
        intra.append(_dot(scores[ci], v_b))
        qhs.append((q_ref[rs(ci), :] * jnp.exp(cum)).astype(BF16))
        decays.append(jnp.exp(cum[c - 1:c, :]))
    outs = []
    for ci in range(n_chunks):
        outs.append(intra[ci] + _dot_nt(qhs[ci], st.astype(BF16)))
        st = st * decays[ci] + kvs[ci]
    fill[3]()
    for ci in range(n_chunks):
        out = outs[ci]
        out = out * lax.rsqrt(jnp.mean(out * out, axis=-1, keepdims=True) + RMS_EPS)
        o_ref[rs(ci), lane0:lane0 + d] = (
            (out * ng * _silu(proj_ref[rs(ci), 3 * d:4 * d])).astype(BF16))
    return st


def _proj_pieces(xb_ref, w_ref, proj_ref, n_pieces):
    kdim, ndim = w_ref.shape
    n_split = 2
    k_split = n_pieces // n_split
    kw = kdim // k_split
    nw = ndim // n_split
    pieces = []
    for ni in range(n_split):
        ns = slice(ni * nw, (ni + 1) * nw)
        for ki in range(k_split):
            ks = slice(ki * kw, (ki + 1) * kw)

            def piece(ns=ns, ks=ks, first=(ki == 0)):
                part = _dot(xb_ref[:, ks], w_ref[ks, ns])
                if first:
                    proj_ref[:, ns] = part
                else:
                    proj_ref[:, ns] += part

            pieces.append(piece)
    return pieces


def _hgrn2_kernel(layer, rows, x_ref, w0_ref, wodd_ref, wnext_ref, lbl_ref, ng_ref, e_ref,
                  m_ref, o_ref, xb_ref, pa_ref, pb_ref, st_ref, qa_ref, ka_ref, xa_ref,
                  qb_ref, kb_ref, xsb_ref):
    s = pl.program_id(1)
    p = pl.program_id(2)
    d = HG_DIM

    @pl.when(p == 0)
    def _():
        xb_ref[...] = x_ref[...].astype(BF16)
        pa_ref[...] = _dot(xb_ref[...], w0_ref[...])

    @pl.when(s == 0)
    def _():
        st_ref[2 * p] = jnp.zeros((d, d), F32)
        st_ref[2 * p + 1] = jnp.zeros((d, d), F32)

    e3 = e_ref[...]
    st_ref[2 * p] = _hg_head(layer, rows, pa_ref, lbl_ref[0], ng_ref[0], e3, m_ref,
                             st_ref[2 * p], o_ref, 0, qa_ref, ka_ref, xa_ref,
                             _proj_pieces(xb_ref, wodd_ref, pb_ref, 4))
    st_ref[2 * p + 1] = _hg_head(layer, rows, pb_ref, lbl_ref[1], ng_ref[1], e3, m_ref,
                                 st_ref[2 * p + 1], o_ref, d, qb_ref, kb_ref, xsb_ref,
                                 _proj_pieces(xb_ref, wnext_ref, pa_ref, 4))


def _hgrn2_mixer(x2d, batch, seq, layer, lb_logits, w_in, norm_g, rows):
    m = x2d.shape[0]
    nst = seq // rows
    d = HG_DIM
    pairs = HG_HEADS // 2
    w = w_in.reshape(D_MODEL, 4, HG_HEADS, d).transpose(2, 0, 1, 3)
    w = w.reshape(HG_HEADS, D_MODEL, 4 * d).astype(BF16)
    lbl = lb_logits.astype(F32).reshape(DEPTH + 1, HG_HEADS, d).transpose(1, 0, 2)
    ng = norm_g.astype(F32).reshape(HG_HEADS, 1, d)
    e_np, m_np = _hg_tables()
    e_all = jnp.asarray(np.concatenate([e_np, e_np, e_np], axis=1), BF16)
    masks = jnp.asarray(m_np, F32)
    return pl.pallas_call(
        functools.partial(_hgrn2_kernel, layer, rows),
        grid=(batch, nst, pairs),
        in_specs=[
            pl.BlockSpec((rows, D_MODEL), lambda b, s, p: (b * nst + s, 0)),
            pl.BlockSpec((None, D_MODEL, 4 * d), lambda b, s, p: (0, 0, 0)),
            pl.BlockSpec((None, D_MODEL, 4 * d), lambda b, s, p: (2 * p + 1, 0, 0)),
            pl.BlockSpec((None, D_MODEL, 4 * d),
                         lambda b, s, p: (jnp.minimum(2 * p + 2, HG_HEADS - 1), 0, 0)),
            pl.BlockSpec((2, DEPTH + 1, d), lambda b, s, p: (p, 0, 0)),
            pl.BlockSpec((2, 1, d), lambda b, s, p: (p, 0, 0)),
            pl.BlockSpec(e_all.shape, lambda b, s, p: (0, 0)),
            pl.BlockSpec(masks.shape, lambda b, s, p: (0, 0, 0)),
        ],
        out_specs=pl.BlockSpec((rows, 2 * d), lambda b, s, p: (b * nst + s, p)),
        out_shape=jax.ShapeDtypeStruct((m, HG_HEADS * d), BF16),
        scratch_shapes=[
            pltpu.VMEM((rows, D_MODEL), BF16),
            pltpu.VMEM((rows, 4 * d), F32),
            pltpu.VMEM((rows, 4 * d), F32),
            pltpu.VMEM((HG_HEADS, d, d), F32),
        ] + 2 * [
            pltpu.VMEM((rows, d), F32),
            pltpu.VMEM((rows, d), F32),
            pltpu.VMEM((rows // (2 * HG_CHUNK), 8 * HG_CHUNK, 2 * d), F32),
        ],
        compiler_params=_params("arbitrary", "arbitrary", "arbitrary"),
        name="hgrn2_mixer",
    )(x2d, w, w, w, lbl, ng, e_all, masks)


_LN_UNROLL = 4


def _ln_rows(ln_rows, n_rows, z_of, g, b, o_ref, o16_ref=None):
    def body(r, carry):
        rs = pl.ds(pl.multiple_of(r * ln_rows, ln_rows), ln_rows)
        y = _layer_norm(z_of(rs), g, b)
        o_ref[rs, :] = y
        if o16_ref is not None:
            o16_ref[rs, :] = y.astype(BF16)
        return carry

    lax.fori_loop(0, n_rows // ln_rows, body, 0, unroll=_LN_UNROLL)


def _outproj_ln_kernel(rows, ln_rows, y_ref, w_ref, x_ref, g_ref, b_ref, o_ref, acc_ref):
    acc_ref[...] = _dot(y_ref[...], w_ref[...])
    _ln_rows(ln_rows, rows, lambda rs: ALPHA * x_ref[rs, :] + acc_ref[rs, :],
             g_ref[...], b_ref[...], o_ref)


def _outproj_ln(y2d, w_out, x2d, g, b, rows, ln_rows):
    m = x2d.shape[0]
    return pl.pallas_call(
        functools.partial(_outproj_ln_kernel, rows, ln_rows),
        grid=(m // rows,),
        in_specs=[
            pl.BlockSpec((rows, D_MODEL), lambda i: (i, 0)),
            pl.BlockSpec((D_MODEL, D_MODEL), lambda i: (0, 0)),
            pl.BlockSpec((rows, D_MODEL), lambda i: (i, 0)),
            pl.BlockSpec((1, D_MODEL), lambda i: (0, 0)),
            pl.BlockSpec((1, D_MODEL), lambda i: (0, 0)),
        ],
        out_specs=pl.BlockSpec((rows, D_MODEL), lambda i: (i, 0)),
        out_shape=jax.ShapeDtypeStruct((m, D_MODEL), F32),
        scratch_shapes=[pltpu.VMEM((rows, D_MODEL), F32)],
        compiler_params=_params("arbitrary"),
        name="outproj_ln",
    )(y2d, w_out.astype(BF16), x2d, g.astype(F32).reshape(1, D_MODEL),
      b.astype(F32).reshape(1, D_MODEL))


_HALO = BF16_SUBLANE_TILE
_CONV_ROWS = 64


def _ffn_kernel(rows, ln_rows, seq_tiles, with_bf16, x_ref, xp_ref, wa_ref, wb_ref, cw_ref,
                cb_ref, wd_ref, g_ref, b_ref, o_ref, *rest):
    o16_ref = rest[0] if with_bf16 else None
    xe_ref, a_ref, bb_ref, hm_ref, acc_ref = rest[-5:]
    i = pl.program_id(0)
    j = pl.program_id(1)

    @pl.when(j == 0)
    def _():
        keep = (i % seq_tiles) != 0
        xe_ref[0:_HALO, :] = jnp.where(keep, xp_ref[...], 0.0).astype(BF16)
        xe_ref[_HALO:, :] = x_ref[...].astype(BF16)
        acc_ref[...] = jnp.zeros_like(acc_ref)

    a_ref[...] = _dot(xe_ref[...], wa_ref[...])
    bb_ref[...] = _dot(xe_ref[_HALO:, :], wb_ref[...])
    cw = cw_ref[...]
    cb = cb_ref[...]
    for r0 in range(0, rows, _CONV_ROWS):
        a_0 = a_ref[_HALO + r0:_HALO + r0 + _CONV_ROWS, :]
        a_1 = a_ref[_HALO + r0 - 1:_HALO + r0 - 1 + _CONV_ROWS, :]
        a_2 = a_ref[_HALO + r0 - 2:_HALO + r0 - 2 + _CONV_ROWS, :]
        conv = cw[0:1] * a_2 + cw[1:2] * a_1 + cw[2:3] * a_0 + cb
        hm = _silu(conv) * bb_ref[r0:r0 + _CONV_ROWS, :]
        hm_ref[r0:r0 + _CONV_ROWS, :] = hm.astype(BF16)
    acc_ref[...] += _dot(hm_ref[...], wd_ref[...])

    @pl.when(j == pl.num_programs(1) - 1)
    def _():
        _ln_rows(ln_rows, rows, lambda rs: ALPHA * x_ref[rs, :] + acc_ref[rs, :],
                 g_ref[...], b_ref[...], o_ref, o16_ref)


def _conv_ffn_ln(x2d, seq, w_up, conv_w, conv_b, w_down, g, b, rows, cols, ln_rows, with_bf16):
    m = x2d.shape[0]
    nj = D_FF // cols
    halo_blocks = rows // _HALO
    w_up = w_up.astype(BF16)
    out_spec = pl.BlockSpec((rows, D_MODEL), lambda i, j: (i, 0))
    out_f32 = jax.ShapeDtypeStruct((m, D_MODEL), F32)
    return pl.pallas_call(
        functools.partial(_ffn_kernel, rows, ln_rows, seq // rows, with_bf16),
        grid=(m // rows, nj),
        in_specs=[
            pl.BlockSpec((rows, D_MODEL), lambda i, j: (i, 0)),
            pl.BlockSpec((_HALO, D_MODEL), lambda i, j: (jnp.maximum(i * halo_blocks - 1, 0), 0)),
            pl.BlockSpec((D_MODEL, cols), lambda i, j: (0, j)),
            pl.BlockSpec((D_MODEL, cols), lambda i, j: (0, nj + j)),
            pl.BlockSpec((CONV_WIDTH, cols), lambda i, j: (0, j)),
            pl.BlockSpec((1, cols), lambda i, j: (0, j)),
            pl.BlockSpec((cols, D_MODEL), lambda i, j: (j, 0)),
            pl.BlockSpec((1, D_MODEL), lambda i, j: (0, 0)),
            pl.BlockSpec((1, D_MODEL), lambda i, j: (0, 0)),
        ],
        out_specs=(out_spec, out_spec) if with_bf16 else out_spec,
        out_shape=((out_f32, jax.ShapeDtypeStruct((m, D_MODEL), BF16)) if with_bf16
                   else out_f32),
        scratch_shapes=[
            pltpu.VMEM((_HALO + rows, D_MODEL), BF16),
            pltpu.VMEM((_HALO + rows, cols), F32),
            pltpu.VMEM((rows, cols), F32),
            pltpu.VMEM((rows, cols), BF16),
            pltpu.VMEM((rows, D_MODEL), F32),
        ],
        compiler_params=_params("arbitrary", "arbitrary"),
        name="conv_ffn_ln",
    )(x2d, x2d, w_up, w_up, conv_w.astype(F32), conv_b.astype(F32).reshape(1, D_FF),
      w_down.astype(BF16), g.astype(F32).reshape(1, D_MODEL), b.astype(F32).reshape(1, D_MODEL))


def _gmlp_kernel(rows, ln_rows, xc_ref, xn_ref, wu_ref, wv_ref, lng_ref, lnb_ref, ws_ref,
                 bias_ref, o_ref, zv_ref, zu_ref, vn_ref, gate_ref, wc_ref):
    i = pl.program_id(0)
    c = SG_CHUNK
    d = SG_DIM

    @pl.when(i == 0)
    def _():
        causal = (lax.broadcasted_iota(jnp.int32, (c, c), 0)
                  >= lax.broadcasted_iota(jnp.int32, (c, c), 1))
        for g in range(SG_GROUPS):
            wc_ref[g] = jnp.where(causal, ws_ref[g], 0.0).astype(BF16)
        zv_ref[...] = _dot(xc_ref[...], wv_ref[...])

    zu_ref[...] = _dot(xc_ref[...], wu_ref[...])
    lng = lng_ref[...]
    lnb = lnb_ref[...]
    for r0 in range(0, rows, ln_rows):
        rs = slice(r0, r0 + ln_rows)
        vn_ref[rs, :] = _layer_norm(_gelu_exact(zv_ref[rs, :]), lng, lnb).astype(BF16)
    for r0 in range(0, rows, c):
        rs = slice(r0, r0 + c)
        for g in range(SG_GROUPS):
            gs = slice(g * d, (g + 1) * d)
            gate_ref[rs, gs] = _dot(wc_ref[g], vn_ref[rs, gs]) + bias_ref[:, gs]
    zv_ref[...] = _dot(xn_ref[...], wv_ref[...])
    for r0 in range(0, rows, ln_rows):
        rs = slice(r0, r0 + ln_rows)
        o_ref[rs, :] = (_gelu_exact(zu_ref[rs, :]) * gate_ref[rs, :]).astype(BF16)


def _gmlp_mixer(x16, w_in, ln_g, ln_b, w_s, b_s, rows, ln_rows):
    m = x16.shape[0]
    n_tiles = m // rows
    width = SG_GROUPS * SG_DIM
    bias = jnp.broadcast_to(b_s.astype(F32).T[:, :, None], (SG_CHUNK, SG_GROUPS, SG_DIM))
    bias = bias.reshape(SG_CHUNK, width)
    w_in = w_in.astype(BF16)
    once = pl.Buffered(1)
    return pl.pallas_call(
        functools.partial(_gmlp_kernel, rows, ln_rows),
        grid=(n_tiles,),
        in_specs=[
            pl.BlockSpec((rows, D_MODEL), lambda i: (i, 0)),
            pl.BlockSpec((rows, D_MODEL), lambda i: (jnp.minimum(i + 1, n_tiles - 1), 0)),
            pl.BlockSpec((D_MODEL, width), lambda i: (0, 0), pipeline_mode=once),
            pl.BlockSpec((D_MODEL, width), lambda i: (0, 1), pipeline_mode=once),
            pl.BlockSpec((1, width), lambda i: (0, 0)),
            pl.BlockSpec((1, width), lambda i: (0, 0)),
            pl.BlockSpec((SG_GROUPS, SG_CHUNK, SG_CHUNK), lambda i: (0, 0, 0), pipeline_mode=once),
            pl.BlockSpec((SG_CHUNK, width), lambda i: (0, 0), pipeline_mode=once),
        ],
        out_specs=pl.BlockSpec((rows, width), lambda i: (i, 0)),
        out_shape=jax.ShapeDtypeStruct((m, width), BF16),
        scratch_shapes=[
            pltpu.VMEM((rows, width), F32),
            pltpu.VMEM((rows, width), F32),
            pltpu.VMEM((rows, width), BF16),
            pltpu.VMEM((rows, width), F32),
            pltpu.VMEM((SG_GROUPS, SG_CHUNK, SG_CHUNK), BF16),
        ],
        compiler_params=_params("arbitrary"),
        name="gmlp_mixer",
    )(x16, x16, w_in, w_in, ln_g.astype(F32).reshape(1, width),
      ln_b.astype(F32).reshape(1, width), w_s.astype(F32), bias)


def kernel(x, lb_logits, hg_w_in, hg_norm_g, hg_w_out, sg_w_in, sg_ln_g, sg_ln_b, sg_w_s,
           sg_b_s, sg_w_out, ffn_w_up, ffn_conv_w, ffn_conv_b, ffn_w_down, ln1_g, ln1_b,
           ln2_g, ln2_b):
    batch, seq, _ = x.shape
    t = _tiles(batch, seq)
    h = x.astype(F32).reshape(batch * seq, D_MODEL)
    h16 = None
    for layer in range(DEPTH):
        occ = layer // N_MIXERS
        if layer % N_MIXERS == 0:
            mixed = _hgrn2_mixer(h, batch, seq, layer, lb_logits, hg_w_in[occ], hg_norm_g[occ],
                                 t["hg_rows"])
            w_out = hg_w_out[occ]
        else:
            mixed = _gmlp_mixer(h16 if h16 is not None else h.astype(BF16), sg_w_in[occ],
                                sg_ln_g[occ], sg_ln_b[occ], sg_w_s[occ], sg_b_s[occ],
                                t["sg_rows"], t["ln_rows"])
            w_out = sg_w_out[occ]
        h = _outproj_ln(mixed, w_out, h, ln1_g[layer], ln1_b[layer], t["out_rows"], t["ln_rows"])
        next_is_gmlp = layer + 1 < DEPTH and (layer + 1) % N_MIXERS == 1
        res = _conv_ffn_ln(h, seq, ffn_w_up[layer], ffn_conv_w[layer], ffn_conv_b[layer],
                           ffn_w_down[layer], ln2_g[layer], ln2_b[layer],
                           t["ffn_rows"], t["ffn_cols"], t["ln_rows"], next_is_gmlp)
        h, h16 = res if next_is_gmlp else (res, None)
    return h.reshape(batch, seq, D_MODEL).astype(x.dtype)
```

```python
import functools

import numpy as np
import jax
import jax.numpy as jnp
from jax import lax
from jax.experimental import pallas as pl
from jax.experimental.pallas import tpu as pltpu

D_MODEL = 2048
DEPTH = 2
N_MIXERS = 2
HG_HEADS = 16
HG_DIM = 128
HG_CHUNK = 64
SG_GROUPS = 16
SG_DIM = 128
SG_CHUNK = 128
D_FF = 5632
CONV_WIDTH = 3
ALPHA = (2 * DEPTH) ** 0.25
LN_EPS = 1e-5
RMS_EPS = 1e-6

F32 = jnp.float32
BF16 = jnp.bfloat16

V7X_VMEM_LIMIT_BYTES = 56 * 1024 * 1024
BF16_SUBLANE_TILE = 16


def _tiles(batch, seq):
    rows = batch * seq
    t = 512
    assert seq % t == 0 and rows % t == 0 and D_FF % 512 == 0
    return dict(hg_rows=t, out_rows=256, ffn_rows=t, ffn_cols=512, sg_rows=t, ln_rows=32)


def _sigmoid_pair(z):
    e = jnp.exp(-jnp.abs(z))
    r = 1.0 / (1.0 + e)
    er = e * r
    pos = z >= 0
    return jnp.where(pos, r, er), jnp.where(pos, er, r)


def _silu(z):
    return z / (1.0 + jnp.exp(-z))


def _gelu_exact(z):
    return 0.5 * z * (1.0 + lax.erf(z * (2.0 ** -0.5)))


def _layer_norm(z, g, b):
    mu = jnp.mean(z, axis=-1, keepdims=True)
    zc = z - mu
    var = jnp.mean(zc * zc, axis=-1, keepdims=True)
    return zc * lax.rsqrt(var + LN_EPS) * g + b


def _dot(a, b):
    return jnp.dot(a, b, preferred_element_type=F32)


def _dot_nt(a, b):
    return lax.dot_general(a, b, (((1,), (1,)), ((), ())), preferred_element_type=F32)


def _dot_tn(a, b):
    return lax.dot_general(a, b, (((0,), (0,)), ((), ())), preferred_element_type=F32)


def _params(*semantics):
    return pltpu.CompilerParams(dimension_semantics=semantics,
                                vmem_limit_bytes=V7X_VMEM_LIMIT_BYTES)


_HG_LEVELS = (32, 16, 8, 4, 2, 1)
_N_LEV = len(_HG_LEVELS)


def _hg_tables():
    c = HG_CHUNK
    e = np.zeros((8 * c, c), np.float32)
    masks = np.zeros((_N_LEV + 1, c, c), np.float32)
    t = np.arange(c)
    for li, h in enumerate(_HG_LEVELS):
        for tt in range(c):
            pos = tt % (2 * h)
            m = tt - pos + h
            if pos >= h:
                e[li * c + tt, m:tt + 1] = 1.0
            else:
                e[li * c + tt, tt + 1:m] = 1.0
        blk = t // (2 * h)
        upper = (t % (2 * h)) >= h
        masks[li] = (blk[:, None] == blk[None, :]) & upper[:, None] & (~upper)[None, :]
    masks[_N_LEV] = np.eye(c)
    e[6 * c:7 * c] = np.tril(np.ones((c, c)))
    e[7 * c:8 * c] = np.triu(np.ones((c, c)), 1)
    return e, masks


def _hg_head(layer, rows, proj_ref, lbl, ng, e3, m_ref, st, o_ref, lane0, q_ref, k_ref,
             xs_ref, fill):
    c = HG_CHUNK
    d = HG_DIM
    n_chunks = rows // c
    lrows = [lbl[i:i + 1, :] for i in range(DEPTH + 1)]
    mx = functools.reduce(jnp.maximum, lrows)
    ex = [jnp.exp(r - mx) for r in lrows]
    lb = sum(ex[:layer + 1]) / sum(ex)
    one_m_lb = 1.0 - lb

    def rs(ci):
        return slice(ci * c, (ci + 1) * c)

    def xs(ci, block):
        return xs_ref[ci // 2, block * c:(block + 1) * c, (ci % 2) * d:(ci % 2 + 1) * d]

    fill = iter(fill)

    def emit_fill():
        next(fill)()

    emit_fill()
    for pair in range(n_chunks // 2):
        lfs = []
        for ci in (2 * pair, 2 * pair + 1):
            q_ref[rs(ci), :] = _silu(proj_ref[rs(ci), 0:d])
            sig, sig_neg = _sigmoid_pair(proj_ref[rs(ci), d:2 * d])
            lfs.append(jnp.log(lb + one_m_lb * sig))
            k_ref[rs(ci), :] = one_m_lb * sig_neg
        lf2 = jnp.concatenate(lfs, axis=1)
        hi = lf2.astype(BF16)
        r1 = lf2 - hi.astype(F32)
        mid = r1.astype(BF16)
        lo = (r1 - mid.astype(F32)).astype(BF16)
        xs_ref[pair] = _dot(e3, jnp.concatenate([hi, mid, lo], axis=0))
    emit_fill()
    scores = []
    for ci in range(n_chunks):
        q = q_ref[rs(ci), :].astype(BF16)
        k = k_ref[rs(ci), :].astype(BF16)
        sc = _dot_nt(q, k) * m_ref[_N_LEV]
        for li in range(_N_LEV):
            w = jnp.exp(xs(ci, li)).astype(BF16)
            sc = sc + _dot_nt(q * w, k * w) * m_ref[li]
        scores.append(sc.astype(BF16))
    emit_fill()
    intra, kvs, qhs, decays = [], [], [], []
    for ci in range(n_chunks):
        v_b = proj_ref[rs(ci), 2 * d:3 * d].astype(BF16)
        cum = xs(ci, 6)
        kvs.append(_dot_tn(v_b, (k_ref[rs(ci), :] * jnp.exp(xs(ci, 7))).astype(BF16)))
        intra.append(_dot(scores[ci], v_b))
        qhs.append((q_ref[rs(ci), :] * jnp.exp(cum)).astype(BF16))
        decays.append(jnp.exp(cum[c - 1:c, :]))
    outs = []
    for ci in range(n_chunks):
        outs.append(intra[ci] + _dot_nt(qhs[ci], st.astype(BF16)))
        st = st * decays[ci] + kvs[ci]
    emit_fill()
    for ci in range(n_chunks):
        out = outs[ci]
        out = out * lax.rsqrt(jnp.mean(out * out, axis=-1, keepdims=True) + RMS_EPS)
        o_ref[rs(ci), lane0:lane0 + d] = (
            (out * ng * _silu(proj_ref[rs(ci), 3 * d:4 * d])).astype(BF16))
    return st


def _hg_n_fill(rows):
    return 4


def _proj_pieces(xb_ref, w_refs, proj_ref, n_pieces):
    kdim, d = w_refs[0].shape
    n_split = 2
    k_split = n_pieces // n_split
    kw = kdim // k_split
    pieces = []
    for ni in range(n_split):
        ns = slice(ni * 2 * d, (ni + 1) * 2 * d)
        for ki in range(k_split):
            ks = slice(ki * kw, (ki + 1) * kw)

            def piece(ni=ni, ns=ns, ks=ks, first=(ki == 0)):
                w = jnp.concatenate([w_refs[2 * ni][ks, :], w_refs[2 * ni + 1][ks, :]], axis=1)
                part = _dot(xb_ref[:, ks], w)
                if first:
                    proj_ref[:, ns] = part
                else:
                    proj_ref[:, ns] += part

            pieces.append(piece)
    return pieces


def _hgrn2_kernel(layer, rows, x_ref, *refs):
    w0_refs, wodd_refs, wnext_refs = refs[0:4], refs[4:8], refs[8:12]
    (lbl_ref, ng_ref, e_ref, m_ref, o_ref, xb_ref, pa_ref, pb_ref, st_ref, qa_ref, ka_ref,
     xa_ref, qb_ref, kb_ref, xsb_ref) = refs[12:]
    s = pl.program_id(1)
    p = pl.program_id(2)
    d = HG_DIM

    @pl.when(p == 0)
    def _():
        xb_ref[...] = x_ref[...].astype(BF16)
        pa_ref[...] = _dot(xb_ref[...], jnp.concatenate([r[...] for r in w0_refs], axis=1))

    @pl.when(s == 0)
    def _():
        st_ref[2 * p] = jnp.zeros((d, d), F32)
        st_ref[2 * p + 1] = jnp.zeros((d, d), F32)

    e3 = e_ref[...]
    st_ref[2 * p] = _hg_head(layer, rows, pa_ref, lbl_ref[0], ng_ref[0], e3, m_ref,
                             st_ref[2 * p], o_ref, 0, qa_ref, ka_ref, xa_ref,
                             _proj_pieces(xb_ref, wodd_refs, pb_ref, _hg_n_fill(rows)))
    st_ref[2 * p + 1] = _hg_head(layer, rows, pb_ref, lbl_ref[1], ng_ref[1], e3, m_ref,
                                 st_ref[2 * p + 1], o_ref, d, qb_ref, kb_ref, xsb_ref,
                                 _proj_pieces(xb_ref, wnext_refs, pa_ref, _hg_n_fill(rows)))


def _hgrn2_mixer(x2d, batch, seq, layer, lb_logits, w_in, norm_g, rows):
    m = x2d.shape[0]
    nst = seq // rows
    d = HG_DIM
    pairs = HG_HEADS // 2
    w = w_in.astype(BF16)

    def head_cols(head_of):
        return [pl.BlockSpec((D_MODEL, d),
                             lambda b, s, p, sec=sec: (0, sec * HG_HEADS + head_of(p)))
                for sec in range(4)]

    lbl = lb_logits.astype(F32).reshape(DEPTH + 1, HG_HEADS, d).transpose(1, 0, 2)
    ng = norm_g.astype(F32).reshape(HG_HEADS, 1, d)
    e_np, m_np = _hg_tables()
    e_all = jnp.asarray(np.concatenate([e_np, e_np, e_np], axis=1), BF16)
    masks = jnp.asarray(m_np, F32)
    return pl.pallas_call(
        functools.partial(_hgrn2_kernel, layer, rows),
        grid=(batch, nst, pairs),
        in_specs=[
            pl.BlockSpec((rows, D_MODEL), lambda b, s, p: (b * nst + s, 0)),
            *head_cols(lambda p: 0),
            *head_cols(lambda p: 2 * p + 1),
            *head_cols(lambda p: jnp.minimum(2 * p + 2, HG_HEADS - 1)),
            pl.BlockSpec((2, DEPTH + 1, d), lambda b, s, p: (p, 0, 0)),
            pl.BlockSpec((2, 1, d), lambda b, s, p: (p, 0, 0)),
            pl.BlockSpec(e_all.shape, lambda b, s, p: (0, 0)),
            pl.BlockSpec(masks.shape, lambda b, s, p: (0, 0, 0)),
        ],
        out_specs=pl.BlockSpec((rows, 2 * d), lambda b, s, p: (b * nst + s, p)),
        out_shape=jax.ShapeDtypeStruct((m, HG_HEADS * d), BF16),
        scratch_shapes=[
            pltpu.VMEM((rows, D_MODEL), BF16),
            pltpu.VMEM((rows, 4 * d), F32),
            pltpu.VMEM((rows, 4 * d), F32),
            pltpu.VMEM((HG_HEADS, d, d), F32),
        ] + 2 * [
            pltpu.VMEM((rows, d), F32),
            pltpu.VMEM((rows, d), F32),
            pltpu.VMEM((rows // (2 * HG_CHUNK), 8 * HG_CHUNK, 2 * d), F32),
        ],
        compiler_params=_params("arbitrary", "arbitrary", "arbitrary"),
        name="hgrn2_mixer",
    )(x2d, *(12 * [w]), lbl, ng, e_all, masks)


_LN_UNROLL = 4


def _ln_rows(ln_rows, n_rows, z_of, g, b, o_ref, o16_ref=None):
    def body(r, carry):
        rs = pl.ds(pl.multiple_of(r * ln_rows, ln_rows), ln_rows)
        y = _layer_norm(z_of(rs), g, b)
        o_ref[rs, :] = y
        if o16_ref is not None:
            o16_ref[rs, :] = y.astype(BF16)
        return carry

    lax.fori_loop(0, n_rows // ln_rows, body, 0, unroll=_LN_UNROLL)


def _outproj_ln_kernel(rows, ln_rows, y0_ref, yodd_ref, ynext_ref, w_ref, x_ref, g_ref, b_ref,
                       o_ref, acca_ref, accb_ref):
    t = pl.program_id(0)

    @pl.when(t == 0)
    def _():
        acca_ref[...] = _dot(y0_ref[...], w_ref[...])

    g = g_ref[...]
    b = b_ref[...]

    def ln(acc_ref, row0):
        for r0 in range(0, rows, ln_rows):
            xs = slice(row0 + r0, row0 + r0 + ln_rows)
            z = ALPHA * x_ref[xs, :] + acc_ref[r0:r0 + ln_rows, :]
            o_ref[xs, :] = _layer_norm(z, g, b)

    accb_ref[...] = _dot(yodd_ref[...], w_ref[...])
    ln(acca_ref, 0)
    acca_ref[...] = _dot(ynext_ref[...], w_ref[...])
    ln(accb_ref, rows)


def _outproj_ln(y2d, w_out, x2d, g, b, rows, ln_rows):
    m = x2d.shape[0]
    n_sub = m // rows
    once = pl.Buffered(1)
    return pl.pallas_call(
        functools.partial(_outproj_ln_kernel, rows, ln_rows),
        grid=(n_sub // 2,),
        in_specs=[
            pl.BlockSpec((rows, D_MODEL), lambda t: (0, 0), pipeline_mode=once),
            pl.BlockSpec((rows, D_MODEL), lambda t: (2 * t + 1, 0)),
            pl.BlockSpec((rows, D_MODEL), lambda t: (jnp.minimum(2 * t + 2, n_sub - 1), 0)),
            pl.BlockSpec((D_MODEL, D_MODEL), lambda t: (0, 0), pipeline_mode=once),
            pl.BlockSpec((2 * rows, D_MODEL), lambda t: (t, 0)),
            pl.BlockSpec((1, D_MODEL), lambda t: (0, 0)),
            pl.BlockSpec((1, D_MODEL), lambda t: (0, 0)),
        ],
        out_specs=pl.BlockSpec((2 * rows, D_MODEL), lambda t: (t, 0)),
        out_shape=jax.ShapeDtypeStruct((m, D_MODEL), F32),
        scratch_shapes=[pltpu.VMEM((rows, D_MODEL), F32), pltpu.VMEM((rows, D_MODEL), F32)],
        compiler_params=_params("arbitrary"),
        name="outproj_ln",
    )(y2d, y2d, y2d, w_out.astype(BF16), x2d, g.astype(F32).reshape(1, D_MODEL),
      b.astype(F32).reshape(1, D_MODEL))


_HALO = BF16_SUBLANE_TILE
_CONV_ROWS = 64


def _ffn_kernel(rows, ln_rows, seq_tiles, with_bf16, x_ref, xp_ref, wa_ref, wb_ref, cw_ref,
                cb_ref, wd_ref, g_ref, b_ref, o_ref, *rest):
    o16_ref = rest[0] if with_bf16 else None
    xe_ref, a_ref, bb_ref, hm_ref, acc_ref = rest[-5:]
    i = pl.program_id(0)
    j = pl.program_id(1)

    @pl.when(j == 0)
    def _():
        keep = (i % seq_tiles) != 0
        xe_ref[0:_HALO, :] = jnp.where(keep, xp_ref[...], 0.0).astype(BF16)
        xe_ref[_HALO:, :] = x_ref[...].astype(BF16)
        acc_ref[...] = jnp.zeros_like(acc_ref)

    a_ref[...] = _dot(xe_ref[...], wa_ref[...])
    bb_ref[...] = _dot(xe_ref[_HALO:, :], wb_ref[...])
    cw = cw_ref[...]
    cb = cb_ref[...]
    for r0 in range(0, rows, _CONV_ROWS):
        a_0 = a_ref[_HALO + r0:_HALO + r0 + _CONV_ROWS, :]
        a_1 = a_ref[_HALO + r0 - 1:_HALO + r0 - 1 + _CONV_ROWS, :]
        a_2 = a_ref[_HALO + r0 - 2:_HALO + r0 - 2 + _CONV_ROWS, :]
        conv = cw[0:1] * a_2 + cw[1:2] * a_1 + cw[2:3] * a_0 + cb
        hm = _silu(conv) * bb_ref[r0:r0 + _CONV_ROWS, :]
        hm_ref[r0:r0 + _CONV_ROWS, :] = hm.astype(BF16)
    acc_ref[...] += _dot(hm_ref[...], wd_ref[...])

    @pl.when(j == pl.num_programs(1) - 1)
    def _():
        _ln_rows(ln_rows, rows, lambda rs: ALPHA * x_ref[rs, :] + acc_ref[rs, :],
                 g_ref[...], b_ref[...], o_ref, o16_ref)


def _conv_ffn_ln(x2d, seq, layer, w_up, conv_w, conv_b, w_down, g, b, rows, cols, ln_rows,
                 with_bf16):
    m = x2d.shape[0]
    nj = D_FF // cols
    halo_blocks = rows // _HALO
    out_spec = pl.BlockSpec((rows, D_MODEL), lambda i, j: (i, 0))
    out_f32 = jax.ShapeDtypeStruct((m, D_MODEL), F32)
    return pl.pallas_call(
        functools.partial(_ffn_kernel, rows, ln_rows, seq // rows, with_bf16),
        grid=(m // rows, nj),
        in_specs=[
            pl.BlockSpec((rows, D_MODEL), lambda i, j: (i, 0)),
            pl.BlockSpec((_HALO, D_MODEL), lambda i, j: (jnp.maximum(i * halo_blocks - 1, 0), 0)),
            pl.BlockSpec((None, D_MODEL, cols), lambda i, j: (layer, 0, j)),
            pl.BlockSpec((None, D_MODEL, cols), lambda i, j: (layer, 0, nj + j)),
            pl.BlockSpec((CONV_WIDTH, cols), lambda i, j: (0, j)),
            pl.BlockSpec((1, cols), lambda i, j: (0, j)),
            pl.BlockSpec((None, cols, D_MODEL), lambda i, j: (layer, j, 0)),
            pl.BlockSpec((1, D_MODEL), lambda i, j: (0, 0)),
            pl.BlockSpec((1, D_MODEL), lambda i, j: (0, 0)),
        ],
        out_specs=(out_spec, out_spec) if with_bf16 else out_spec,
        out_shape=((out_f32, jax.ShapeDtypeStruct((m, D_MODEL), BF16)) if with_bf16
                   else out_f32),
        scratch_shapes=[
            pltpu.VMEM((_HALO + rows, D_MODEL), BF16),
            pltpu.VMEM((_HALO + rows, cols), F32),
            pltpu.VMEM((rows, cols), F32),
            pltpu.VMEM((rows, cols), BF16),
            pltpu.VMEM((rows, D_MODEL), F32),
        ],
        compiler_params=_params("arbitrary", "arbitrary"),
        name="conv_ffn_ln",
    )(x2d, x2d, w_up, w_up, conv_w.astype(F32), conv_b.astype(F32).reshape(1, D_FF),
      w_down, g.astype(F32).reshape(1, D_MODEL), b.astype(F32).reshape(1, D_MODEL))


def _gmlp_kernel(rows, ln_rows, xc_ref, xn_ref, wu_ref, wv_ref, lng_ref, lnb_ref, ws_ref,
                 bias_ref, o_ref, zv_ref, zu_ref, vn_ref, gate_ref, wc_ref):
    i = pl.program_id(0)
    c = SG_CHUNK
    d = SG_DIM

    @pl.when(i == 0)
    def _():
        causal = (lax.broadcasted_iota(jnp.int32, (c, c), 0)
                  >= lax.broadcasted_iota(jnp.int32, (c, c), 1))
        for g in range(SG_GROUPS):
            wc_ref[g] = jnp.where(causal, ws_ref[g], 0.0).astype(BF16)
        zv_ref[...] = _dot(xc_ref[...], wv_ref[...])

    zu_ref[...] = _dot(xc_ref[...], wu_ref[...])
    lng = lng_ref[...]
    lnb = lnb_ref[...]
    for r0 in range(0, rows, ln_rows):
        rs = slice(r0, r0 + ln_rows)
        vn_ref[rs, :] = _layer_norm(_gelu_exact(zv_ref[rs, :]), lng, lnb).astype(BF16)
    for r0 in range(0, rows, c):
        rs = slice(r0, r0 + c)
        for g in range(SG_GROUPS):
            gs = slice(g * d, (g + 1) * d)
            gate_ref[rs, gs] = _dot(wc_ref[g], vn_ref[rs, gs]) + bias_ref[:, gs]
    zv_ref[...] = _dot(xn_ref[...], wv_ref[...])
    for r0 in range(0, rows, ln_rows):
        rs = slice(r0, r0 + ln_rows)
        o_ref[rs, :] = (_gelu_exact(zu_ref[rs, :]) * gate_ref[rs, :]).astype(BF16)


def _gmlp_mixer(x16, w_in, ln_g, ln_b, w_s, b_s, rows, ln_rows):
    m = x16.shape[0]
    n_tiles = m // rows
    width = SG_GROUPS * SG_DIM
    bias = jnp.broadcast_to(b_s.astype(F32).T[:, :, None], (SG_CHUNK, SG_GROUPS, SG_DIM))
    bias = bias.reshape(SG_CHUNK, width)
    w_in = w_in.astype(BF16)
    once = pl.Buffered(1)
    return pl.pallas_call(
        functools.partial(_gmlp_kernel, rows, ln_rows),
        grid=(n_tiles,),
        in_specs=[
            pl.BlockSpec((rows, D_MODEL), lambda i: (i, 0)),
            pl.BlockSpec((rows, D_MODEL), lambda i: (jnp.minimum(i + 1, n_tiles - 1), 0)),
            pl.BlockSpec((D_MODEL, width), lambda i: (0, 0), pipeline_mode=once),
            pl.BlockSpec((D_MODEL, width), lambda i: (0, 1), pipeline_mode=once),
            pl.BlockSpec((1, width), lambda i: (0, 0)),
            pl.BlockSpec((1, width), lambda i: (0, 0)),
            pl.BlockSpec((SG_GROUPS, SG_CHUNK, SG_CHUNK), lambda i: (0, 0, 0), pipeline_mode=once),
            pl.BlockSpec((SG_CHUNK, width), lambda i: (0, 0), pipeline_mode=once),
        ],
        out_specs=pl.BlockSpec((rows, width), lambda i: (i, 0)),
        out_shape=jax.ShapeDtypeStruct((m, width), BF16),
        scratch_shapes=[
            pltpu.VMEM((rows, width), F32),
            pltpu.VMEM((rows, width), F32),
            pltpu.VMEM((rows, width), BF16),
            pltpu.VMEM((rows, width), F32),
            pltpu.VMEM((SG_GROUPS, SG_CHUNK, SG_CHUNK), BF16),
        ],
        compiler_params=_params("arbitrary"),
        name="gmlp_mixer",
    )(x16, x16, w_in, w_in, ln_g.astype(F32).reshape(1, width),
      ln_b.astype(F32).reshape(1, width), w_s.astype(F32), bias)


def kernel(x, lb_logits, hg_w_in, hg_norm_g, hg_w_out, sg_w_in, sg_ln_g, sg_ln_b, sg_w_s,
           sg_b_s, sg_w_out, ffn_w_up, ffn_conv_w, ffn_conv_b, ffn_w_down, ln1_g, ln1_b,
           ln2_g, ln2_b):
    batch, seq, _ = x.shape
    t = _tiles(batch, seq)
    h = x.astype(F32).reshape(batch * seq, D_MODEL)
    h16 = None
    w_up16 = ffn_w_up.astype(BF16)
    w_down16 = ffn_w_down.astype(BF16)
    for layer in range(DEPTH):
        occ = layer // N_MIXERS
        if layer % N_MIXERS == 0:
            mixed = _hgrn2_mixer(h, batch, seq, layer, lb_logits, hg_w_in[occ], hg_norm_g[occ],
                                 t["hg_rows"])
            w_out = hg_w_out[occ]
        else:
            mixed = _gmlp_mixer(h16 if h16 is not None else h.astype(BF16), sg_w_in[occ],
                                sg_ln_g[occ], sg_ln_b[occ], sg_w_s[occ], sg_b_s[occ],
                                t["sg_rows"], t["ln_rows"])
            w_out = sg_w_out[occ]
        h = _outproj_ln(mixed, w_out, h, ln1_g[layer], ln1_b[layer], t["out_rows"], t["ln_rows"])
        next_is_gmlp = layer + 1 < DEPTH and (layer + 1) % N_MIXERS == 1
        res = _conv_ffn_ln(h, seq, layer, w_up16, ffn_conv_w[layer], ffn_conv_b[layer],
                           w_down16, ln2_g[layer], ln2_b[layer],
                           t["ffn_rows"], t["ffn_cols"], t["ln_rows"], next_is_gmlp)
        h, h16 = res if next_is_gmlp else (res, None)
    return h.reshape(batch, seq, D_MODEL).astype(x.dtype)
```

```python
import functools

import numpy as np
import jax
import jax.numpy as jnp
from jax import lax
from jax.experimental import pallas as pl
from jax.experimental.pallas import tpu as pltpu

D_MODEL = 2048
DEPTH = 2
N_MIXERS = 2
HG_HEADS = 16
HG_DIM = 128
HG_CHUNK = 64
SG_GROUPS = 16
SG_DIM = 128
SG_CHUNK = 128
D_FF = 5632
CONV_WIDTH = 3
ALPHA = (2 * DEPTH) ** 0.25
LN_EPS = 1e-5
RMS_EPS = 1e-6

F32 = jnp.float32
BF16 = jnp.bfloat16

V7X_VMEM_LIMIT_BYTES = 56 * 1024 * 1024
BF16_SUBLANE_TILE = 16


def _tiles(batch, seq):
    rows = batch * seq
    t = 512
    assert seq % t == 0 and rows % t == 0 and D_FF % 512 == 0
    return dict(hg_rows=2 * t, out_rows=256, ffn_rows=t, ffn_cols=512, sg_rows=t, ln_rows=32)


def _sigmoid_pair(z):
    e = jnp.exp(-jnp.abs(z))
    r = 1.0 / (1.0 + e)
    er = e * r
    pos = z >= 0
    return jnp.where(pos, r, er), jnp.where(pos, er, r)


def _silu(z):
    return z / (1.0 + jnp.exp(-z))


def _gelu_exact(z):
    return 0.5 * z * (1.0 + lax.erf(z * (2.0 ** -0.5)))


def _layer_norm(z, g, b):
    mu = jnp.mean(z, axis=-1, keepdims=True)
    zc = z - mu
    var = jnp.mean(zc * zc, axis=-1, keepdims=True)
    return zc * lax.rsqrt(var + LN_EPS) * g + b


def _dot(a, b):
    return jnp.dot(a, b, preferred_element_type=F32)


def _dot_nt(a, b):
    return lax.dot_general(a, b, (((1,), (1,)), ((), ())), preferred_element_type=F32)


def _dot_tn(a, b):
    return lax.dot_general(a, b, (((0,), (0,)), ((), ())), preferred_element_type=F32)


def _params(*semantics):
    return pltpu.CompilerParams(dimension_semantics=semantics,
                                vmem_limit_bytes=V7X_VMEM_LIMIT_BYTES)


_HG_LEVELS = (32, 16, 8, 4, 2, 1)
_N_LEV = len(_HG_LEVELS)


def _hg_tables():
    c = HG_CHUNK
    e = np.zeros((8 * c, c), np.float32)
    masks = np.zeros((_N_LEV + 1, c, c), np.float32)
    t = np.arange(c)
    for li, h in enumerate(_HG_LEVELS):
        for tt in range(c):
            pos = tt % (2 * h)
            m = tt - pos + h
            if pos >= h:
                e[li * c + tt, m:tt + 1] = 1.0
            else:
                e[li * c + tt, tt + 1:m] = 1.0
        blk = t // (2 * h)
        upper = (t % (2 * h)) >= h
        masks[li] = (blk[:, None] == blk[None, :]) & upper[:, None] & (~upper)[None, :]
    masks[_N_LEV] = np.eye(c)
    e[6 * c:7 * c] = np.tril(np.ones((c, c)))
    e[7 * c:8 * c] = np.triu(np.ones((c, c)), 1)
    return e, masks


def _hg_head(layer, rows, proj_ref, lbl, ng, e3, m_ref, st, o_ref, lane0, q_ref, k_ref,
             xs_ref, fill):
    c = HG_CHUNK
    d = HG_DIM
    n_chunks = rows // c
    lrows = [lbl[i:i + 1, :] for i in range(DEPTH + 1)]
    mx = functools.reduce(jnp.maximum, lrows)
    ex = [jnp.exp(r - mx) for r in lrows]
    lb = sum(ex[:layer + 1]) / sum(ex)
    one_m_lb = 1.0 - lb

    def rs(ci):
        return slice(ci * c, (ci + 1) * c)

    def xs(ci, block):
        return xs_ref[ci // 2, block * c:(block + 1) * c, (ci % 2) * d:(ci % 2 + 1) * d]

    fill = iter(fill)

    def emit_fill():
        next(fill)()

    emit_fill()
    for pair in range(n_chunks // 2):
        lfs = []
        for ci in (2 * pair, 2 * pair + 1):
            q_ref[rs(ci), :] = _silu(proj_ref[rs(ci), 0:d])
            sig, sig_neg = _sigmoid_pair(proj_ref[rs(ci), d:2 * d])
            lfs.append(jnp.log(lb + one_m_lb * sig))
            k_ref[rs(ci), :] = one_m_lb * sig_neg
        lf2 = jnp.concatenate(lfs, axis=1)
        hi = lf2.astype(BF16)
        r1 = lf2 - hi.astype(F32)
        mid = r1.astype(BF16)
        lo = (r1 - mid.astype(F32)).astype(BF16)
        xs_ref[pair] = _dot(e3, jnp.concatenate([hi, mid, lo], axis=0))
    for _ in range(_hg_n_fill(rows) - 1):
        emit_fill()
    scores = []
    for ci in range(n_chunks):
        q = q_ref[rs(ci), :].astype(BF16)
        k = k_ref[rs(ci), :].astype(BF16)
        sc = _dot_nt(q, k) * m_ref[_N_LEV]
        for li in range(_N_LEV):
            w = jnp.exp(xs(ci, li)).astype(BF16)
            sc = sc + _dot_nt(q * w, k * w) * m_ref[li]
        scores.append(sc.astype(BF16))
    intra, kvs, qhs, decays = [], [], [], []
    for ci in range(n_chunks):
        v_b = proj_ref[rs(ci), 2 * d:3 * d].astype(BF16)
        cum = xs(ci, 6)
        kvs.append(_dot_tn(v_b, (k_ref[rs(ci), :] * jnp.exp(xs(ci, 7))).astype(BF16)))
        intra.append(_dot(scores[ci], v_b))
        qhs.append((q_ref[rs(ci), :] * jnp.exp(cum)).astype(BF16))
        decays.append(jnp.exp(cum[c - 1:c, :]))
    outs = []
    for ci in range(n_chunks):
        outs.append(intra[ci] + _dot_nt(qhs[ci], st.astype(BF16)))
        st = st * decays[ci] + kvs[ci]
    for ci in range(n_chunks):
        out = outs[ci]
        out = out * lax.rsqrt(jnp.mean(out * out, axis=-1, keepdims=True) + RMS_EPS)
        o_ref[rs(ci), lane0:lane0 + d] = (
            (out * ng * _silu(proj_ref[rs(ci), 3 * d:4 * d])).astype(BF16))
    return st


def _hg_n_fill(rows):
    return 4


def _proj_pieces(xb_ref, w_refs, proj_ref, n_pieces):
    kdim, d = w_refs[0].shape
    n_split = 2
    k_split = n_pieces // n_split
    kw = kdim // k_split
    pieces = []
    for ni in range(n_split):
        ns = slice(ni * 2 * d, (ni + 1) * 2 * d)
        for ki in range(k_split):
            ks = slice(ki * kw, (ki + 1) * kw)

            def piece(ni=ni, ns=ns, ks=ks, first=(ki == 0)):
                w = jnp.concatenate([w_refs[2 * ni][ks, :], w_refs[2 * ni + 1][ks, :]], axis=1)
                part = _dot(xb_ref[:, ks], w)
                if first:
                    proj_ref[:, ns] = part
                else:
                    proj_ref[:, ns] += part

            pieces.append(piece)
    return pieces


def _hgrn2_kernel(layer, rows, x_ref, *refs):
    w0_refs, wodd_refs, wnext_refs = refs[0:4], refs[4:8], refs[8:12]
    (lbl_ref, ng_ref, e_ref, m_ref, o_ref, xb_ref, pa_ref, pb_ref, st_ref, qa_ref, ka_ref,
     xa_ref, qb_ref, kb_ref, xsb_ref) = refs[12:]
    s = pl.program_id(1)
    p = pl.program_id(2)
    d = HG_DIM

    @pl.when(p == 0)
    def _():
        xb_ref[...] = x_ref[...].astype(BF16)
        pa_ref[...] = _dot(xb_ref[...], jnp.concatenate([r[...] for r in w0_refs], axis=1))

    @pl.when(s == 0)
    def _():
        st_ref[2 * p] = jnp.zeros((d, d), F32)
        st_ref[2 * p + 1] = jnp.zeros((d, d), F32)

    e3 = e_ref[...]
    st_ref[2 * p] = _hg_head(layer, rows, pa_ref, lbl_ref[0], ng_ref[0], e3, m_ref,
                             st_ref[2 * p], o_ref, 0, qa_ref, ka_ref, xa_ref,
                             _proj_pieces(xb_ref, wodd_refs, pb_ref, _hg_n_fill(rows)))
    st_ref[2 * p + 1] = _hg_head(layer, rows, pb_ref, lbl_ref[1], ng_ref[1], e3, m_ref,
                                 st_ref[2 * p + 1], o_ref, d, qb_ref, kb_ref, xsb_ref,
                                 _proj_pieces(xb_ref, wnext_refs, pa_ref, _hg_n_fill(rows)))


def _hgrn2_mixer(x2d, batch, seq, layer, lb_logits, w_in, norm_g, rows):
    m = x2d.shape[0]
    nst = seq // rows
    d = HG_DIM
    pairs = HG_HEADS // 2
    w = w_in.astype(BF16)

    def head_cols(head_of):
        return [pl.BlockSpec((D_MODEL, d),
                             lambda b, s, p, sec=sec: (0, sec * HG_HEADS + head_of(p)))
                for sec in range(4)]

    lbl = lb_logits.astype(F32).reshape(DEPTH + 1, HG_HEADS, d).transpose(1, 0, 2)
    ng = norm_g.astype(F32).reshape(HG_HEADS, 1, d)
    e_np, m_np = _hg_tables()
    e_all = jnp.asarray(np.concatenate([e_np, e_np, e_np], axis=1), BF16)
    masks = jnp.asarray(m_np, F32)
    return pl.pallas_call(
        functools.partial(_hgrn2_kernel, layer, rows),
        grid=(batch, nst, pairs),
        in_specs=[
            pl.BlockSpec((rows, D_MODEL), lambda b, s, p: (b * nst + s, 0)),
            *head_cols(lambda p: 0),
            *head_cols(lambda p: 2 * p + 1),
            *head_cols(lambda p: jnp.minimum(2 * p + 2, HG_HEADS - 1)),
            pl.BlockSpec((2, DEPTH + 1, d), lambda b, s, p: (p, 0, 0)),
            pl.BlockSpec((2, 1, d), lambda b, s, p: (p, 0, 0)),
            pl.BlockSpec(e_all.shape, lambda b, s, p: (0, 0)),
            pl.BlockSpec(masks.shape, lambda b, s, p: (0, 0, 0)),
        ],
        out_specs=pl.BlockSpec((rows, 2 * d), lambda b, s, p: (b * nst + s, p)),
        out_shape=jax.ShapeDtypeStruct((m, HG_HEADS * d), BF16),
        scratch_shapes=[
            pltpu.VMEM((rows, D_MODEL), BF16),
            pltpu.VMEM((rows, 4 * d), F32),
            pltpu.VMEM((rows, 4 * d), F32),
            pltpu.VMEM((HG_HEADS, d, d), F32),
        ] + 2 * [
            pltpu.VMEM((rows, d), F32),
            pltpu.VMEM((rows, d), F32),
            pltpu.VMEM((rows // (2 * HG_CHUNK), 8 * HG_CHUNK, 2 * d), F32),
        ],
        compiler_params=_params("arbitrary", "arbitrary", "arbitrary"),
        name="hgrn2_mixer",
    )(x2d, *(12 * [w]), lbl, ng, e_all, masks)


_LN_UNROLL = 4


def _ln_rows(ln_rows, n_rows, z_of, g, b, o_ref, o16_ref=None):
    def body(r, carry):
        rs = pl.ds(pl.multiple_of(r * ln_rows, ln_rows), ln_rows)
        y = _layer_norm(z_of(rs), g, b)
        o_ref[rs, :] = y
        if o16_ref is not None:
            o16_ref[rs, :] = y.astype(BF16)
        return carry

    lax.fori_loop(0, n_rows // ln_rows, body, 0, unroll=_LN_UNROLL)


def _outproj_ln_kernel(rows, ln_rows, y0_ref, yodd_ref, ynext_ref, w_ref, x_ref, g_ref, b_ref,
                       o_ref, acca_ref, accb_ref):
    t = pl.program_id(0)

    @pl.when(t == 0)
    def _():
        acca_ref[...] = _dot(y0_ref[...], w_ref[...])

    g = g_ref[...]
    b = b_ref[...]

    def ln(acc_ref, row0):
        for r0 in range(0, rows, ln_rows):
            xs = slice(row0 + r0, row0 + r0 + ln_rows)
            z = ALPHA * x_ref[xs, :] + acc_ref[r0:r0 + ln_rows, :]
            o_ref[xs, :] = _layer_norm(z, g, b)

    accb_ref[...] = _dot(yodd_ref[...], w_ref[...])
    ln(acca_ref, 0)
    acca_ref[...] = _dot(ynext_ref[...], w_ref[...])
    ln(accb_ref, rows)


def _outproj_ln(y2d, w_out, x2d, g, b, rows, ln_rows):
    m = x2d.shape[0]
    n_sub = m // rows
    once = pl.Buffered(1)
    return pl.pallas_call(
        functools.partial(_outproj_ln_kernel, rows, ln_rows),
        grid=(n_sub // 2,),
        in_specs=[
            pl.BlockSpec((rows, D_MODEL), lambda t: (0, 0), pipeline_mode=once),
            pl.BlockSpec((rows, D_MODEL), lambda t: (2 * t + 1, 0)),
            pl.BlockSpec((rows, D_MODEL), lambda t: (jnp.minimum(2 * t + 2, n_sub - 1), 0)),
            pl.BlockSpec((D_MODEL, D_MODEL), lambda t: (0, 0), pipeline_mode=once),
            pl.BlockSpec((2 * rows, D_MODEL), lambda t: (t, 0)),
            pl.BlockSpec((1, D_MODEL), lambda t: (0, 0)),
            pl.BlockSpec((1, D_MODEL), lambda t: (0, 0)),
        ],
        out_specs=pl.BlockSpec((2 * rows, D_MODEL), lambda t: (t, 0)),
        out_shape=jax.ShapeDtypeStruct((m, D_MODEL), F32),
        scratch_shapes=[pltpu.VMEM((rows, D_MODEL), F32), pltpu.VMEM((rows, D_MODEL), F32)],
        compiler_params=_params("arbitrary"),
        name="outproj_ln",
    )(y2d, y2d, y2d, w_out.astype(BF16), x2d, g.astype(F32).reshape(1, D_MODEL),
      b.astype(F32).reshape(1, D_MODEL))


_HALO = BF16_SUBLANE_TILE
_CONV_ROWS = 64


def _ffn_kernel(rows, ln_rows, seq_tiles, with_bf16, x_ref, xp_ref, wa_ref, wb_ref, cw_ref,
                cb_ref, wd_ref, g_ref, b_ref, o_ref, *rest):
    o16_ref = rest[0] if with_bf16 else None
    xe_ref, a_ref, bb_ref, hm_ref, acc_ref = rest[-5:]
    i = pl.program_id(0)
    j = pl.program_id(1)

    @pl.when(j == 0)
    def _():
        keep = (i % seq_tiles) != 0
        xe_ref[0:_HALO, :] = jnp.where(keep, xp_ref[...], 0.0).astype(BF16)
        xe_ref[_HALO:, :] = x_ref[...].astype(BF16)
        acc_ref[...] = jnp.zeros_like(acc_ref)

    a_ref[...] = _dot(xe_ref[...], wa_ref[...])
    bb_ref[...] = _dot(xe_ref[_HALO:, :], wb_ref[...])
    cw = cw_ref[...]
    cb = cb_ref[...]
    for r0 in range(0, rows, _CONV_ROWS):
        a_0 = a_ref[_HALO + r0:_HALO + r0 + _CONV_ROWS, :]
        a_1 = a_ref[_HALO + r0 - 1:_HALO + r0 - 1 + _CONV_ROWS, :]
        a_2 = a_ref[_HALO + r0 - 2:_HALO + r0 - 2 + _CONV_ROWS, :]
        conv = cw[0:1] * a_2 + cw[1:2] * a_1 + cw[2:3] * a_0 + cb
        hm = _silu(conv) * bb_ref[r0:r0 + _CONV_ROWS, :]
        hm_ref[r0:r0 + _CONV_ROWS, :] = hm.astype(BF16)
    acc_ref[...] += _dot(hm_ref[...], wd_ref[...])

    @pl.when(j == pl.num_programs(1) - 1)
    def _():
        _ln_rows(ln_rows, rows, lambda rs: ALPHA * x_ref[rs, :] + acc_ref[rs, :],
                 g_ref[...], b_ref[...], o_ref, o16_ref)


def _conv_ffn_ln(x2d, seq, layer, w_up, conv_w, conv_b, w_down, g, b, rows, cols, ln_rows,
                 with_bf16):
    m = x2d.shape[0]
    nj = D_FF // cols
    halo_blocks = rows // _HALO
    out_spec = pl.BlockSpec((rows, D_MODEL), lambda i, j: (i, 0))
    out_f32 = jax.ShapeDtypeStruct((m, D_MODEL), F32)
    return pl.pallas_call(
        functools.partial(_ffn_kernel, rows, ln_rows, seq // rows, with_bf16),
        grid=(m // rows, nj),
        in_specs=[
            pl.BlockSpec((rows, D_MODEL), lambda i, j: (i, 0)),
            pl.BlockSpec((_HALO, D_MODEL), lambda i, j: (jnp.maximum(i * halo_blocks - 1, 0), 0)),
            pl.BlockSpec((None, D_MODEL, cols), lambda i, j: (layer, 0, j)),
            pl.BlockSpec((None, D_MODEL, cols), lambda i, j: (layer, 0, nj + j)),
            pl.BlockSpec((CONV_WIDTH, cols), lambda i, j: (0, j)),
            pl.BlockSpec((1, cols), lambda i, j: (0, j)),
            pl.BlockSpec((None, cols, D_MODEL), lambda i, j: (layer, j, 0)),
            pl.BlockSpec((1, D_MODEL), lambda i, j: (0, 0)),
            pl.BlockSpec((1, D_MODEL), lambda i, j: (0, 0)),
        ],
        out_specs=(out_spec, out_spec) if with_bf16 else out_spec,
        out_shape=((out_f32, jax.ShapeDtypeStruct((m, D_MODEL), BF16)) if with_bf16
                   else out_f32),
        scratch_shapes=[
            pltpu.VMEM((_HALO + rows, D_MODEL), BF16),
            pltpu.VMEM((_HALO + rows, cols), F32),
            pltpu.VMEM((rows, cols), F32),
            pltpu.VMEM((rows, cols), BF16),
            pltpu.VMEM((rows, D_MODEL), F32),
        ],
        compiler_params=_params("arbitrary", "arbitrary"),
        name="conv_ffn_ln",
    )(x2d, x2d, w_up, w_up, conv_w.astype(F32), conv_b.astype(F32).reshape(1, D_FF),
      w_down, g.astype(F32).reshape(1, D_MODEL), b.astype(F32).reshape(1, D_MODEL))


def _gmlp_kernel(rows, ln_rows, xc_ref, xn_ref, wu_ref, wv_ref, lng_ref, lnb_ref, ws_ref,
                 bias_ref, o_ref, zv_ref, zu_ref, vn_ref, gate_ref, wc_ref):
    i = pl.program_id(0)
    c = SG_CHUNK
    d = SG_DIM

    @pl.when(i == 0)
    def _():
        causal = (lax.broadcasted_iota(jnp.int32, (c, c), 0)
                  >= lax.broadcasted_iota(jnp.int32, (c, c), 1))
        for g in range(SG_GROUPS):
            wc_ref[g] = jnp.where(causal, ws_ref[g], 0.0).astype(BF16)
        zv_ref[...] = _dot(xc_ref[...], wv_ref[...])

    zu_ref[...] = _dot(xc_ref[...], wu_ref[...])
    lng = lng_ref[...]
    lnb = lnb_ref[...]
    for r0 in range(0, rows, ln_rows):
        rs = slice(r0, r0 + ln_rows)
        vn_ref[rs, :] = _layer_norm(_gelu_exact(zv_ref[rs, :]), lng, lnb).astype(BF16)
    for r0 in range(0, rows, c):
        rs = slice(r0, r0 + c)
        for g in range(SG_GROUPS):
            gs = slice(g * d, (g + 1) * d)
            gate_ref[rs, gs] = _dot(wc_ref[g], vn_ref[rs, gs]) + bias_ref[:, gs]
    zv_ref[...] = _dot(xn_ref[...], wv_ref[...])
    for r0 in range(0, rows, ln_rows):
        rs = slice(r0, r0 + ln_rows)
        o_ref[rs, :] = (_gelu_exact(zu_ref[rs, :]) * gate_ref[rs, :]).astype(BF16)


def _gmlp_mixer(x16, w_in, ln_g, ln_b, w_s, b_s, rows, ln_rows):
    m = x16.shape[0]
    n_tiles = m // rows
    width = SG_GROUPS * SG_DIM
    bias = jnp.broadcast_to(b_s.astype(F32).T[:, :, None], (SG_CHUNK, SG_GROUPS, SG_DIM))
    bias = bias.reshape(SG_CHUNK, width)
    w_in = w_in.astype(BF16)
    once = pl.Buffered(1)
    return pl.pallas_call(
        functools.partial(_gmlp_kernel, rows, ln_rows),
        grid=(n_tiles,),
        in_specs=[
            pl.BlockSpec((rows, D_MODEL), lambda i: (i, 0)),
            pl.BlockSpec((rows, D_MODEL), lambda i: (jnp.minimum(i + 1, n_tiles - 1), 0)),
            pl.BlockSpec((D_MODEL, width), lambda i: (0, 0), pipeline_mode=once),
            pl.BlockSpec((D_MODEL, width), lambda i: (0, 1), pipeline_mode=once),
            pl.BlockSpec((1, width), lambda i: (0, 0)),
            pl.BlockSpec((1, width), lambda i: (0, 0)),
            pl.BlockSpec((SG_GROUPS, SG_CHUNK, SG_CHUNK), lambda i: (0, 0, 0), pipeline_mode=once),
            pl.BlockSpec((SG_CHUNK, width), lambda i: (0, 0), pipeline_mode=once),
        ],
        out_specs=pl.BlockSpec((rows, width), lambda i: (i, 0)),
        out_shape=jax.ShapeDtypeStruct((m, width), BF16),
        scratch_shapes=[
            pltpu.VMEM((rows, width), F32),
            pltpu.VMEM((rows, width), F32),
            pltpu.VMEM((rows, width), BF16),
            pltpu.VMEM((rows, width), F32),
            pltpu.VMEM((SG_GROUPS, SG_CHUNK, SG_CHUNK), BF16),
        ],
        compiler_params=_params("arbitrary"),
        name="gmlp_mixer",
    )(x16, x16, w_in, w_in, ln_g.astype(F32).reshape(1, width),
      ln_b.astype(F32).reshape(1, width), w_s.astype(F32), bias)


def kernel(x, lb_logits, hg_w_in, hg_norm_g, hg_w_out, sg_w_in, sg_ln_g, sg_ln_b, sg_w_s,
           sg_b_s, sg_w_out, ffn_w_up, ffn_conv_w, ffn_conv_b, ffn_w_down, ln1_g, ln1_b,
           ln2_g, ln2_b):
    batch, seq, _ = x.shape
    t = _tiles(batch, seq)
    h = x.astype(F32).reshape(batch * seq, D_MODEL)
    h16 = None
    w_up16 = ffn_w_up.astype(BF16)
    w_down16 = ffn_w_down.astype(BF16)
    for layer in range(DEPTH):
        occ = layer // N_MIXERS
        if layer % N_MIXERS == 0:
            mixed = _hgrn2_mixer(h, batch, seq, layer, lb_logits, hg_w_in[occ], hg_norm_g[occ],
                                 t["hg_rows"])
            w_out = hg_w_out[occ]
        else:
            mixed = _gmlp_mixer(h16 if h16 is not None else h.astype(BF16), sg_w_in[occ],
                                sg_ln_g[occ], sg_ln_b[occ], sg_w_s[occ], sg_b_s[occ],
                                t["sg_rows"], t["ln_rows"])
            w_out = sg_w_out[occ]
        h = _outproj_ln(mixed, w_out, h, ln1_g[layer], ln1_b[layer], t["out_rows"], t["ln_rows"])
        next_is_gmlp = layer + 1 < DEPTH and (layer + 1) % N_MIXERS == 1
        res = _conv_ffn_ln(h, seq, layer, w_up16, ffn_conv_w[layer], ffn_conv_b[layer],
                           w_down16, ln2_g[layer], ln2_b[layer],
                           t["ffn_rows"], t["ffn_cols"], t["ln_rows"], next_is_gmlp)
        h, h16 = res if next_is_gmlp else (res, None)
    return h.reshape(batch, seq, D_MODEL).astype(x.dtype)
```

```python
import functools

import numpy as np
import jax
import jax.numpy as jnp
from jax import lax
from jax.experimental import pallas as pl
from jax.experimental.pallas import tpu as pltpu

D_MODEL = 2048
DEPTH = 2
N_MIXERS = 2
HG_HEADS = 16
HG_DIM = 128
HG_CHUNK = 64
SG_GROUPS = 16
SG_DIM = 128
SG_CHUNK = 128
D_FF = 5632
CONV_WIDTH = 3
ALPHA = (2 * DEPTH) ** 0.25
LN_EPS = 1e-5
RMS_EPS = 1e-6

F32 = jnp.float32
BF16 = jnp.bfloat16

V7X_VMEM_LIMIT_BYTES = 56 * 1024 * 1024
BF16_SUBLANE_TILE = 16


def _tiles(batch, seq):
    rows = batch * seq
    t = 512
    assert seq % t == 0 and rows % t == 0 and D_FF % 512 == 0
    return dict(hg_rows=2 * t, out_rows=256, ffn_rows=t, ffn_cols=512, sg_rows=t, ln_rows=32)


def _sigmoid_pair(z):
    e = jnp.exp(-jnp.abs(z))
    r = 1.0 / (1.0 + e)
    er = e * r
    pos = z >= 0
    return jnp.where(pos, r, er), jnp.where(pos, er, r)


def _silu(z):
    return z / (1.0 + jnp.exp(-z))


def _gelu_exact(z):
    return 0.5 * z * (1.0 + lax.erf(z * (2.0 ** -0.5)))


def _layer_norm(z, g, b):
    mu = jnp.mean(z, axis=-1, keepdims=True)
    zc = z - mu
    var = jnp.mean(zc * zc, axis=-1, keepdims=True)
    return zc * lax.rsqrt(var + LN_EPS) * g + b


def _dot(a, b):
    return jnp.dot(a, b, preferred_element_type=F32)


def _dot_nt(a, b):
    return lax.dot_general(a, b, (((1,), (1,)), ((), ())), preferred_element_type=F32)


def _dot_tn(a, b):
    return lax.dot_general(a, b, (((0,), (0,)), ((), ())), preferred_element_type=F32)


def _params(*semantics):
    return pltpu.CompilerParams(dimension_semantics=semantics,
                                vmem_limit_bytes=V7X_VMEM_LIMIT_BYTES)


_HG_LEVELS = (32, 16, 8, 4, 2, 1)
_N_LEV = len(_HG_LEVELS)


def _hg_tables():
    c = HG_CHUNK
    e = np.zeros((8 * c, c), np.float32)
    masks = np.zeros((_N_LEV + 1, c, c), np.float32)
    t = np.arange(c)
    for li, h in enumerate(_HG_LEVELS):
        for tt in range(c):
            pos = tt % (2 * h)
            m = tt - pos + h
            if pos >= h:
                e[li * c + tt, m:tt + 1] = 1.0
            else:
                e[li * c + tt, tt + 1:m] = 1.0
        blk = t // (2 * h)
        upper = (t % (2 * h)) >= h
        masks[li] = (blk[:, None] == blk[None, :]) & upper[:, None] & (~upper)[None, :]
    masks[_N_LEV] = np.eye(c)
    e[6 * c:7 * c] = np.tril(np.ones((c, c)))
    e[7 * c:8 * c] = np.triu(np.ones((c, c)), 1)
    return e, masks


def _hg_head(layer, rows, proj_ref, lbl, ng, e3, m_ref, st, o_ref, lane0, q_ref, k_ref,
             xs_ref, fill):
    c = HG_CHUNK
    d = HG_DIM
    n_chunks = rows // c
    lrows = [lbl[i:i + 1, :] for i in range(DEPTH + 1)]
    mx = functools.reduce(jnp.maximum, lrows)
    ex = [jnp.exp(r - mx) for r in lrows]
    lb = sum(ex[:layer + 1]) / sum(ex)
    one_m_lb = 1.0 - lb

    def rs(ci):
        return slice(ci * c, (ci + 1) * c)

    def xs(ci, block):
        return xs_ref[ci // 2, block * c:(block + 1) * c, (ci % 2) * d:(ci % 2 + 1) * d]

    fill = iter(fill)

    def emit_fill():
        next(fill)()

    emit_fill()
    for pair in range(n_chunks // 2):
        lfs = []
        for ci in (2 * pair, 2 * pair + 1):
            q_ref[rs(ci), :] = _silu(proj_ref[rs(ci), 0:d])
            sig, sig_neg = _sigmoid_pair(proj_ref[rs(ci), d:2 * d])
            lfs.append(jnp.log(lb + one_m_lb * sig))
            k_ref[rs(ci), :] = one_m_lb * sig_neg
        lf2 = jnp.concatenate(lfs, axis=1)
        hi = lf2.astype(BF16)
        r1 = lf2 - hi.astype(F32)
        mid = r1.astype(BF16)
        lo = (r1 - mid.astype(F32)).astype(BF16)
        xs_ref[pair] = _dot(e3, jnp.concatenate([hi, mid, lo], axis=0))
    for _ in range(_hg_n_fill(rows) - 1):
        emit_fill()
    scores = []
    for ci in range(n_chunks):
        q = q_ref[rs(ci), :].astype(BF16)
        k = k_ref[rs(ci), :].astype(BF16)
        sc = _dot_nt(q, k) * m_ref[_N_LEV]
        for li in range(_N_LEV):
            w = jnp.exp(xs(ci, li)).astype(BF16)
            sc = sc + _dot_nt(q * w, k * w) * m_ref[li]
        scores.append(sc.astype(BF16))
    intra, kvs, qhs, decays = [], [], [], []
    for ci in range(n_chunks):
        v_b = proj_ref[rs(ci), 2 * d:3 * d].astype(BF16)
        cum = xs(ci, 6)
        kvs.append(_dot_tn(v_b, (k_ref[rs(ci), :] * jnp.exp(xs(ci, 7))).astype(BF16)))
        intra.append(_dot(scores[ci], v_b))
        qhs.append((q_ref[rs(ci), :] * jnp.exp(cum)).astype(BF16))
        decays.append(jnp.exp(cum[c - 1:c, :]))
    outs = []
    for ci in range(n_chunks):
        outs.append(intra[ci] + _dot_nt(qhs[ci], st.astype(BF16)))
        st = st * decays[ci] + kvs[ci]
    for ci in range(n_chunks):
        out = outs[ci]
        out = out * lax.rsqrt(jnp.mean(out * out, axis=-1, keepdims=True) + RMS_EPS)
        o_ref[rs(ci), lane0:lane0 + d] = (
            (out * ng * _silu(proj_ref[rs(ci), 3 * d:4 * d])).astype(BF16))
    return st


def _hg_n_fill(rows):
    return 4


def _proj_pieces(xb_ref, w_refs, proj_ref, n_pieces):
    kdim, d = w_refs[0].shape
    n_split = 2
    k_split = n_pieces // n_split
    kw = kdim // k_split
    pieces = []
    for ni in range(n_split):
        ns = slice(ni * 2 * d, (ni + 1) * 2 * d)
        for ki in range(k_split):
            ks = slice(ki * kw, (ki + 1) * kw)

            def piece(ni=ni, ns=ns, ks=ks, first=(ki == 0)):
                w = jnp.concatenate([w_refs[2 * ni][ks, :], w_refs[2 * ni + 1][ks, :]], axis=1)
                part = _dot(xb_ref[:, ks], w)
                if first:
                    proj_ref[:, ns] = part
                else:
                    proj_ref[:, ns] += part

            pieces.append(piece)
    return pieces


def _hgrn2_kernel(layer, rows, x_ref, *refs):
    w0_refs, wodd_refs, wnext_refs = refs[0:4], refs[4:8], refs[8:12]
    (lbl_ref, ng_ref, e_ref, m_ref, o_ref, xb_ref, pa_ref, pb_ref, st_ref, qa_ref, ka_ref,
     xa_ref, qb_ref, kb_ref, xsb_ref) = refs[12:]
    s = pl.program_id(1)
    p = pl.program_id(2)
    d = HG_DIM

    @pl.when(p == 0)
    def _():
        xb_ref[...] = x_ref[...].astype(BF16)
        pa_ref[...] = _dot(xb_ref[...], jnp.concatenate([r[...] for r in w0_refs], axis=1))

    @pl.when(s == 0)
    def _():
        st_ref[2 * p] = jnp.zeros((d, d), F32)
        st_ref[2 * p + 1] = jnp.zeros((d, d), F32)

    e3 = e_ref[...]
    st_ref[2 * p] = _hg_head(layer, rows, pa_ref, lbl_ref[0], ng_ref[0], e3, m_ref,
                             st_ref[2 * p], o_ref, 0, qa_ref, ka_ref, xa_ref,
                             _proj_pieces(xb_ref, wodd_refs, pb_ref, _hg_n_fill(rows)))
    st_ref[2 * p + 1] = _hg_head(layer, rows, pb_ref, lbl_ref[1], ng_ref[1], e3, m_ref,
                                 st_ref[2 * p + 1], o_ref, d, qb_ref, kb_ref, xsb_ref,
                                 _proj_pieces(xb_ref, wnext_refs, pa_ref, _hg_n_fill(rows)))


def _hgrn2_mixer(x2d, batch, seq, layer, lb_logits, w_in, norm_g, rows):
    m = x2d.shape[0]
    nst = seq // rows
    d = HG_DIM
    pairs = HG_HEADS // 2
    w = w_in.astype(BF16)

    def head_cols(head_of):
        return [pl.BlockSpec((D_MODEL, d),
                             lambda b, s, p, sec=sec: (0, sec * HG_HEADS + head_of(p)))
                for sec in range(4)]

    lbl = lb_logits.astype(F32).reshape(DEPTH + 1, HG_HEADS, d).transpose(1, 0, 2)
    ng = norm_g.astype(F32).reshape(HG_HEADS, 1, d)
    e_np, m_np = _hg_tables()
    e_all = jnp.asarray(np.concatenate([e_np, e_np, e_np], axis=1), BF16)
    masks = jnp.asarray(m_np, F32)
    return pl.pallas_call(
        functools.partial(_hgrn2_kernel, layer, rows),
        grid=(batch, nst, pairs),
        in_specs=[
            pl.BlockSpec((rows, D_MODEL), lambda b, s, p: (b * nst + s, 0)),
            *head_cols(lambda p: 0),
            *head_cols(lambda p: 2 * p + 1),
            *head_cols(lambda p: jnp.minimum(2 * p + 2, HG_HEADS - 1)),
            pl.BlockSpec((2, DEPTH + 1, d), lambda b, s, p: (p, 0, 0)),
            pl.BlockSpec((2, 1, d), lambda b, s, p: (p, 0, 0)),
            pl.BlockSpec(e_all.shape, lambda b, s, p: (0, 0)),
            pl.BlockSpec(masks.shape, lambda b, s, p: (0, 0, 0)),
        ],
        out_specs=pl.BlockSpec((rows, 2 * d), lambda b, s, p: (b * nst + s, p)),
        out_shape=jax.ShapeDtypeStruct((m, HG_HEADS * d), BF16),
        scratch_shapes=[
            pltpu.VMEM((rows, D_MODEL), BF16),
            pltpu.VMEM((rows, 4 * d), F32),
            pltpu.VMEM((rows, 4 * d), F32),
            pltpu.VMEM((HG_HEADS, d, d), F32),
        ] + 2 * [
            pltpu.VMEM((rows, d), F32),
            pltpu.VMEM((rows, d), F32),
            pltpu.VMEM((rows // (2 * HG_CHUNK), 8 * HG_CHUNK, 2 * d), F32),
        ],
        compiler_params=_params("arbitrary", "arbitrary", "arbitrary"),
        name="hgrn2_mixer",
    )(x2d, *(12 * [w]), lbl, ng, e_all, masks)


_LN_UNROLL = 4


def _ln_rows(ln_rows, n_rows, z_of, g, b, o_ref, o16_ref=None):
    def body(r, carry):
        rs = pl.ds(pl.multiple_of(r * ln_rows, ln_rows), ln_rows)
        y = _layer_norm(z_of(rs), g, b)
        o_ref[rs, :] = y
        if o16_ref is not None:
            o16_ref[rs, :] = y.astype(BF16)
        return carry

    lax.fori_loop(0, n_rows // ln_rows, body, 0, unroll=_LN_UNROLL)


def _outproj_ln_kernel(rows, ln_rows, y0_ref, yodd_ref, ynext_ref, w_ref, x_ref, g_ref, b_ref,
                       o_ref, acca_ref, accb_ref):
    t = pl.program_id(0)

    @pl.when(t == 0)
    def _():
        acca_ref[...] = _dot(y0_ref[...], w_ref[...])

    g = g_ref[...]
    b = b_ref[...]

    def ln(acc_ref, row0):
        for r0 in range(0, rows, ln_rows):
            xs = slice(row0 + r0, row0 + r0 + ln_rows)
            z = ALPHA * x_ref[xs, :] + acc_ref[r0:r0 + ln_rows, :]
            o_ref[xs, :] = _layer_norm(z, g, b)

    accb_ref[...] = _dot(yodd_ref[...], w_ref[...])
    ln(acca_ref, 0)
    acca_ref[...] = _dot(ynext_ref[...], w_ref[...])
    ln(accb_ref, rows)


def _outproj_ln(y2d, w_out, x2d, g, b, rows, ln_rows):
    m = x2d.shape[0]
    n_sub = m // rows
    once = pl.Buffered(1)
    return pl.pallas_call(
        functools.partial(_outproj_ln_kernel, rows, ln_rows),
        grid=(n_sub // 2,),
        in_specs=[
            pl.BlockSpec((rows, D_MODEL), lambda t: (0, 0), pipeline_mode=once),
            pl.BlockSpec((rows, D_MODEL), lambda t: (2 * t + 1, 0)),
            pl.BlockSpec((rows, D_MODEL), lambda t: (jnp.minimum(2 * t + 2, n_sub - 1), 0)),
            pl.BlockSpec((D_MODEL, D_MODEL), lambda t: (0, 0), pipeline_mode=once),
            pl.BlockSpec((2 * rows, D_MODEL), lambda t: (t, 0)),
            pl.BlockSpec((1, D_MODEL), lambda t: (0, 0)),
            pl.BlockSpec((1, D_MODEL), lambda t: (0, 0)),
        ],
        out_specs=pl.BlockSpec((2 * rows, D_MODEL), lambda t: (t, 0)),
        out_shape=jax.ShapeDtypeStruct((m, D_MODEL), F32),
        scratch_shapes=[pltpu.VMEM((rows, D_MODEL), F32), pltpu.VMEM((rows, D_MODEL), F32)],
        compiler_params=_params("arbitrary"),
        name="outproj_ln",
    )(y2d, y2d, y2d, w_out.astype(BF16), x2d, g.astype(F32).reshape(1, D_MODEL),
      b.astype(F32).reshape(1, D_MODEL))


_HALO = BF16_SUBLANE_TILE
_CONV_ROWS = 64


def _ffn_kernel(rows, ln_rows, seq_tiles, nj, with_bf16, x_ref, xres_ref, xp_ref, wa_ref, wb_ref,
                cw_ref, cb_ref, wd_ref, g_ref, b_ref, o_ref, *rest):
    o16_ref = rest[0] if with_bf16 else None
    xe_ref, a_ref, bb_ref, hm_ref, acc_ref = rest[-5:]
    n = pl.program_id(0)
    i = n // nj
    j = n % nj

    @pl.when(j == 0)
    def _():
        keep = (i % seq_tiles) != 0
        xe_ref[0:_HALO, :] = jnp.where(keep, xp_ref[...], 0.0).astype(BF16)
        xe_ref[_HALO:, :] = x_ref[...].astype(BF16)

    @pl.when(n == 0)
    def _():
        hm_ref[...] = jnp.zeros_like(hm_ref)
        acc_ref[...] = jnp.zeros_like(acc_ref)

    acc_ref[...] += _dot(hm_ref[...], wd_ref[...])
    a_ref[...] = _dot(xe_ref[...], wa_ref[...])
    bb_ref[...] = _dot(xe_ref[_HALO:, :], wb_ref[...])
    cw = cw_ref[...]
    cb = cb_ref[...]
    for r0 in range(0, rows, _CONV_ROWS):
        a_0 = a_ref[_HALO + r0:_HALO + r0 + _CONV_ROWS, :]
        a_1 = a_ref[_HALO + r0 - 1:_HALO + r0 - 1 + _CONV_ROWS, :]
        a_2 = a_ref[_HALO + r0 - 2:_HALO + r0 - 2 + _CONV_ROWS, :]
        conv = cw[0:1] * a_2 + cw[1:2] * a_1 + cw[2:3] * a_0 + cb
        hm = _silu(conv) * bb_ref[r0:r0 + _CONV_ROWS, :]
        hm_ref[r0:r0 + _CONV_ROWS, :] = hm.astype(BF16)

    @pl.when((j == 0) & (n > 0))
    def _():
        _ln_rows(ln_rows, rows, lambda rs: ALPHA * xres_ref[rs, :] + acc_ref[rs, :],
                 g_ref[...], b_ref[...], o_ref, o16_ref)
        acc_ref[...] = jnp.zeros_like(acc_ref)


def _conv_ffn_ln(x2d, seq, layer, w_up, conv_w, conv_b, w_down, g, b, rows, cols, ln_rows,
                 with_bf16):
    m = x2d.shape[0]
    nj = D_FF // cols
    n_tiles = m // rows
    halo_blocks = rows // _HALO

    def up_tile(n):
        return jnp.minimum(n // nj, n_tiles - 1)

    def ln_tile(n):
        return jnp.maximum(n // nj - 1, 0)

    out_spec = pl.BlockSpec((rows, D_MODEL), lambda n: (ln_tile(n), 0))
    out_f32 = jax.ShapeDtypeStruct((m, D_MODEL), F32)
    return pl.pallas_call(
        functools.partial(_ffn_kernel, rows, ln_rows, seq // rows, nj, with_bf16),
        grid=(n_tiles * nj + 1,),
        in_specs=[
            pl.BlockSpec((rows, D_MODEL), lambda n: (up_tile(n), 0)),
            pl.BlockSpec((rows, D_MODEL), lambda n: (ln_tile(n), 0)),
            pl.BlockSpec((_HALO, D_MODEL),
                         lambda n: (jnp.maximum(up_tile(n) * halo_blocks - 1, 0), 0)),
            pl.BlockSpec((None, D_MODEL, cols), lambda n: (layer, 0, n % nj)),
            pl.BlockSpec((None, D_MODEL, cols), lambda n: (layer, 0, nj + n % nj)),
            pl.BlockSpec((CONV_WIDTH, cols), lambda n: (0, n % nj)),
            pl.BlockSpec((1, cols), lambda n: (0, n % nj)),
            pl.BlockSpec((None, cols, D_MODEL),
                         lambda n: (layer, jnp.maximum(n - 1, 0) % nj, 0)),
            pl.BlockSpec((1, D_MODEL), lambda n: (0, 0)),
            pl.BlockSpec((1, D_MODEL), lambda n: (0, 0)),
        ],
        out_specs=(out_spec, out_spec) if with_bf16 else out_spec,
        out_shape=((out_f32, jax.ShapeDtypeStruct((m, D_MODEL), BF16)) if with_bf16
                   else out_f32),
        scratch_shapes=[
            pltpu.VMEM((_HALO + rows, D_MODEL), BF16),
            pltpu.VMEM((_HALO + rows, cols), F32),
            pltpu.VMEM((rows, cols), F32),
            pltpu.VMEM((rows, cols), BF16),
            pltpu.VMEM((rows, D_MODEL), F32),
        ],
        compiler_params=_params("arbitrary"),
        name="conv_ffn_ln",
    )(x2d, x2d, x2d, w_up, w_up, conv_w.astype(F32), conv_b.astype(F32).reshape(1, D_FF),
      w_down, g.astype(F32).reshape(1, D_MODEL), b.astype(F32).reshape(1, D_MODEL))


def _gmlp_kernel(rows, ln_rows, xc_ref, xn_ref, wu_ref, wv_ref, lng_ref, lnb_ref, ws_ref,
                 bias_ref, o_ref, zv_ref, zu_ref, vn_ref, gate_ref, wc_ref):
    i = pl.program_id(0)
    c = SG_CHUNK
    d = SG_DIM

    @pl.when(i == 0)
    def _():
        causal = (lax.broadcasted_iota(jnp.int32, (c, c), 0)
                  >= lax.broadcasted_iota(jnp.int32, (c, c), 1))
        for g in range(SG_GROUPS):
            wc_ref[g] = jnp.where(causal, ws_ref[g], 0.0).astype(BF16)
        zv_ref[...] = _dot(xc_ref[...], wv_ref[...])

    zu_ref[...] = _dot(xc_ref[...], wu_ref[...])
    lng = lng_ref[...]
    lnb = lnb_ref[...]
    for r0 in range(0, rows, ln_rows):
        rs = slice(r0, r0 + ln_rows)
        vn_ref[rs, :] = _layer_norm(_gelu_exact(zv_ref[rs, :]), lng, lnb).astype(BF16)
    for r0 in range(0, rows, c):
        rs = slice(r0, r0 + c)
        for g in range(SG_GROUPS):
            gs = slice(g * d, (g + 1) * d)
            gate_ref[rs, gs] = _dot(wc_ref[g], vn_ref[rs, gs]) + bias_ref[:, gs]
    zv_ref[...] = _dot(xn_ref[...], wv_ref[...])
    for r0 in range(0, rows, ln_rows):
        rs = slice(r0, r0 + ln_rows)
        o_ref[rs, :] = (_gelu_exact(zu_ref[rs, :]) * gate_ref[rs, :]).astype(BF16)


def _gmlp_mixer(x16, w_in, ln_g, ln_b, w_s, b_s, rows, ln_rows):
    m = x16.shape[0]
    n_tiles = m // rows
    width = SG_GROUPS * SG_DIM
    bias = jnp.broadcast_to(b_s.astype(F32).T[:, :, None], (SG_CHUNK, SG_GROUPS, SG_DIM))
    bias = bias.reshape(SG_CHUNK, width)
    w_in = w_in.astype(BF16)
    once = pl.Buffered(1)
    return pl.pallas_call(
        functools.partial(_gmlp_kernel, rows, ln_rows),
        grid=(n_tiles,),
        in_specs=[
            pl.BlockSpec((rows, D_MODEL), lambda i: (i, 0)),
            pl.BlockSpec((rows, D_MODEL), lambda i: (jnp.minimum(i + 1, n_tiles - 1), 0)),
            pl.BlockSpec((D_MODEL, width), lambda i: (0, 0), pipeline_mode=once),
            pl.BlockSpec((D_MODEL, width), lambda i: (0, 1), pipeline_mode=once),
            pl.BlockSpec((1, width), lambda i: (0, 0)),
            pl.BlockSpec((1, width), lambda i: (0, 0)),
            pl.BlockSpec((SG_GROUPS, SG_CHUNK, SG_CHUNK), lambda i: (0, 0, 0), pipeline_mode=once),
            pl.BlockSpec((SG_CHUNK, width), lambda i: (0, 0), pipeline_mode=once),
        ],
        out_specs=pl.BlockSpec((rows, width), lambda i: (i, 0)),
        out_shape=jax.ShapeDtypeStruct((m, width), BF16),
        scratch_shapes=[
            pltpu.VMEM((rows, width), F32),
            pltpu.VMEM((rows, width), F32),
            pltpu.VMEM((rows, width), BF16),
            pltpu.VMEM((rows, width), F32),
            pltpu.VMEM((SG_GROUPS, SG_CHUNK, SG_CHUNK), BF16),
        ],
        compiler_params=_params("arbitrary"),
        name="gmlp_mixer",
    )(x16, x16, w_in, w_in, ln_g.astype(F32).reshape(1, width),
      ln_b.astype(F32).reshape(1, width), w_s.astype(F32), bias)


def kernel(x, lb_logits, hg_w_in, hg_norm_g, hg_w_out, sg_w_in, sg_ln_g, sg_ln_b, sg_w_s,
           sg_b_s, sg_w_out, ffn_w_up, ffn_conv_w, ffn_conv_b, ffn_w_down, ln1_g, ln1_b,
           ln2_g, ln2_b):
    batch, seq, _ = x.shape
    t = _tiles(batch, seq)
    h = x.astype(F32).reshape(batch * seq, D_MODEL)
    h16 = None
    w_up16 = ffn_w_up.astype(BF16)
    w_down16 = ffn_w_down.astype(BF16)
    for layer in range(DEPTH):
        occ = layer // N_MIXERS
        if layer % N_MIXERS == 0:
            mixed = _hgrn2_mixer(h, batch, seq, layer, lb_logits, hg_w_in[occ], hg_norm_g[occ],
                                 t["hg_rows"])
            w_out = hg_w_out[occ]
        else:
            mixed = _gmlp_mixer(h16 if h16 is not None else h.astype(BF16), sg_w_in[occ],
                                sg_ln_g[occ], sg_ln_b[occ], sg_w_s[occ], sg_b_s[occ],
                                t["sg_rows"], t["ln_rows"])
            w_out = sg_w_out[occ]
        h = _outproj_ln(mixed, w_out, h, ln1_g[layer], ln1_b[layer], t["out_rows"], t["ln_rows"])
        next_is_gmlp = layer + 1 < DEPTH and (layer + 1) % N_MIXERS == 1
        res = _conv_ffn_ln(h, seq, layer, w_up16, ffn_conv_w[layer], ffn_conv_b[layer],
                           w_down16, ln2_g[layer], ln2_b[layer],
                           t["ffn_rows"], t["ffn_cols"], t["ln_rows"], next_is_gmlp)
        h, h16 = res if next_is_gmlp else (res, None)
    return h.reshape(batch, seq, D_MODEL).astype(x.dtype)
```

```python
import functools

import numpy as np
import jax
import jax.numpy as jnp
from jax import lax
from jax.experimental import pallas as pl
from jax.experimental.pallas import tpu as pltpu

D_MODEL = 2048
DEPTH = 2
N_MIXERS = 2
HG_HEADS = 16
HG_DIM = 128
HG_CHUNK = 64
SG_GROUPS = 16
SG_DIM = 128
SG_CHUNK = 128
D_FF = 5632
CONV_WIDTH = 3
ALPHA = (2 * DEPTH) ** 0.25
LN_EPS = 1e-5
RMS_EPS = 1e-6

F32 = jnp.float32
BF16 = jnp.bfloat16

V7X_VMEM_LIMIT_BYTES = 56 * 1024 * 1024
BF16_SUBLANE_TILE = 16


def _tiles(batch, seq):
    rows = batch * seq
    t = 512
    assert seq % t == 0 and rows % t == 0 and D_FF % 512 == 0
    return dict(hg_rows=2 * t, out_rows=256, ffn_rows=2 * t, ffn_cols=512, sg_rows=t, ln_rows=32)


def _sigmoid_pair(z):
    e = jnp.exp(-jnp.abs(z))
    r = 1.0 / (1.0 + e)
    er = e * r
    pos = z >= 0
    return jnp.where(pos, r, er), jnp.where(pos, er, r)


def _silu(z):
    return z / (1.0 + jnp.exp(-z))


def _gelu_exact(z):
    return 0.5 * z * (1.0 + lax.erf(z * (2.0 ** -0.5)))


def _layer_norm(z, g, b):
    mu = jnp.mean(z, axis=-1, keepdims=True)
    zc = z - mu
    var = jnp.mean(zc * zc, axis=-1, keepdims=True)
    return zc * lax.rsqrt(var + LN_EPS) * g + b


def _dot(a, b):
    return jnp.dot(a, b, preferred_element_type=F32)


def _dot_nt(a, b):
    return lax.dot_general(a, b, (((1,), (1,)), ((), ())), preferred_element_type=F32)


def _dot_tn(a, b):
    return lax.dot_general(a, b, (((0,), (0,)), ((), ())), preferred_element_type=F32)


def _params(*semantics):
    return pltpu.CompilerParams(dimension_semantics=semantics,
                                vmem_limit_bytes=V7X_VMEM_LIMIT_BYTES)


_HG_LEVELS = (32, 16, 8, 4, 2, 1)
_N_LEV = len(_HG_LEVELS)


def _hg_tables():
    c = HG_CHUNK
    e = np.zeros((8 * c, c), np.float32)
    masks = np.zeros((_N_LEV + 1, c, c), np.float32)
    t = np.arange(c)
    for li, h in enumerate(_HG_LEVELS):
        for tt in range(c):
            pos = tt % (2 * h)
            m = tt - pos + h
            if pos >= h:
                e[li * c + tt, m:tt + 1] = 1.0
            else:
                e[li * c + tt, tt + 1:m] = 1.0
        blk = t // (2 * h)
        upper = (t % (2 * h)) >= h
        masks[li] = (blk[:, None] == blk[None, :]) & upper[:, None] & (~upper)[None, :]
    masks[_N_LEV] = np.eye(c)
    e[6 * c:7 * c] = np.tril(np.ones((c, c)))
    e[7 * c:8 * c] = np.triu(np.ones((c, c)), 1)
    return e, masks


def _hg_head(layer, rows, proj_ref, lbl, ng, e3, m_ref, st, o_ref, lane0, q_ref, k_ref,
             xs_ref, fill):
    c = HG_CHUNK
    d = HG_DIM
    n_chunks = rows // c
    lrows = [lbl[i:i + 1, :] for i in range(DEPTH + 1)]
    mx = functools.reduce(jnp.maximum, lrows)
    ex = [jnp.exp(r - mx) for r in lrows]
    lb = sum(ex[:layer + 1]) / sum(ex)
    one_m_lb = 1.0 - lb

    def rs(ci):
        return slice(ci * c, (ci + 1) * c)

    def xs(ci, block):
        return xs_ref[ci // 2, block * c:(block + 1) * c, (ci % 2) * d:(ci % 2 + 1) * d]

    fill = iter(fill)

    def emit_fill():
        next(fill)()

    emit_fill()
    for pair in range(n_chunks // 2):
        lfs = []
        for ci in (2 * pair, 2 * pair + 1):
            q_ref[rs(ci), :] = _silu(proj_ref[rs(ci), 0:d])
            sig, sig_neg = _sigmoid_pair(proj_ref[rs(ci), d:2 * d])
            lfs.append(jnp.log(lb + one_m_lb * sig))
            k_ref[rs(ci), :] = one_m_lb * sig_neg
        lf2 = jnp.concatenate(lfs, axis=1)
        hi = lf2.astype(BF16)
        r1 = lf2 - hi.astype(F32)
        mid = r1.astype(BF16)
        lo = (r1 - mid.astype(F32)).astype(BF16)
        xs_ref[pair] = _dot(e3, jnp.concatenate([hi, mid, lo], axis=0))
    for _ in range(_hg_n_fill(rows) - 1):
        emit_fill()
    scores = []
    for ci in range(n_chunks):
        q = q_ref[rs(ci), :].astype(BF16)
        k = k_ref[rs(ci), :].astype(BF16)
        sc = _dot_nt(q, k) * m_ref[_N_LEV]
        for li in range(_N_LEV):
            w = jnp.exp(xs(ci, li)).astype(BF16)
            sc = sc + _dot_nt(q * w, k * w) * m_ref[li]
        scores.append(sc.astype(BF16))
    intra, kvs, qhs, decays = [], [], [], []
    for ci in range(n_chunks):
        v_b = proj_ref[rs(ci), 2 * d:3 * d].astype(BF16)
        cum = xs(ci, 6)
        kvs.append(_dot_tn(v_b, (k_ref[rs(ci), :] * jnp.exp(xs(ci, 7))).astype(BF16)))
        intra.append(_dot(scores[ci], v_b))
        qhs.append((q_ref[rs(ci), :] * jnp.exp(cum)).astype(BF16))
        decays.append(jnp.exp(cum[c - 1:c, :]))
    outs = []
    for ci in range(n_chunks):
        outs.append(intra[ci] + _dot_nt(qhs[ci], st.astype(BF16)))
        st = st * decays[ci] + kvs[ci]
    for ci in range(n_chunks):
        out = outs[ci]
        out = out * lax.rsqrt(jnp.mean(out * out, axis=-1, keepdims=True) + RMS_EPS)
        o_ref[rs(ci), lane0:lane0 + d] = (
            (out * ng * _silu(proj_ref[rs(ci), 3 * d:4 * d])).astype(BF16))
    return st


def _hg_n_fill(rows):
    return 4


def _proj_pieces(xb_ref, w_refs, proj_ref, n_pieces):
    kdim, d = w_refs[0].shape
    n_split = 2
    k_split = n_pieces // n_split
    kw = kdim // k_split
    pieces = []
    for ni in range(n_split):
        ns = slice(ni * 2 * d, (ni + 1) * 2 * d)
        for ki in range(k_split):
            ks = slice(ki * kw, (ki + 1) * kw)

            def piece(ni=ni, ns=ns, ks=ks, first=(ki == 0)):
                w = jnp.concatenate([w_refs[2 * ni][ks, :], w_refs[2 * ni + 1][ks, :]], axis=1)
                part = _dot(xb_ref[:, ks], w)
                if first:
                    proj_ref[:, ns] = part
                else:
                    proj_ref[:, ns] += part

            pieces.append(piece)
    return pieces


def _hgrn2_kernel(layer, rows, x_ref, *refs):
    w0_refs, wodd_refs, wnext_refs = refs[0:4], refs[4:8], refs[8:12]
    (lbl_ref, ng_ref, e_ref, m_ref, o_ref, xb_ref, pa_ref, pb_ref, st_ref, qa_ref, ka_ref,
     xa_ref, qb_ref, kb_ref, xsb_ref) = refs[12:]
    s = pl.program_id(1)
    p = pl.program_id(2)
    d = HG_DIM

    @pl.when(p == 0)
    def _():
        xb_ref[...] = x_ref[...].astype(BF16)
        pa_ref[...] = _dot(xb_ref[...], jnp.concatenate([r[...] for r in w0_refs], axis=1))

    @pl.when(s == 0)
    def _():
        st_ref[2 * p] = jnp.zeros((d, d), F32)
        st_ref[2 * p + 1] = jnp.zeros((d, d), F32)

    e3 = e_ref[...]
    st_ref[2 * p] = _hg_head(layer, rows, pa_ref, lbl_ref[0], ng_ref[0], e3, m_ref,
                             st_ref[2 * p], o_ref, 0, qa_ref, ka_ref, xa_ref,
                             _proj_pieces(xb_ref, wodd_refs, pb_ref, _hg_n_fill(rows)))
    st_ref[2 * p + 1] = _hg_head(layer, rows, pb_ref, lbl_ref[1], ng_ref[1], e3, m_ref,
                                 st_ref[2 * p + 1], o_ref, d, qb_ref, kb_ref, xsb_ref,
                                 _proj_pieces(xb_ref, wnext_refs, pa_ref, _hg_n_fill(rows)))


def _hgrn2_mixer(x2d, batch, seq, layer, lb_logits, w_in, norm_g, rows):
    m = x2d.shape[0]
    nst = seq // rows
    d = HG_DIM
    pairs = HG_HEADS // 2
    w = w_in.astype(BF16)

    def head_cols(head_of):
        return [pl.BlockSpec((D_MODEL, d),
                             lambda b, s, p, sec=sec: (0, sec * HG_HEADS + head_of(p)))
                for sec in range(4)]

    lbl = lb_logits.astype(F32).reshape(DEPTH + 1, HG_HEADS, d).transpose(1, 0, 2)
    ng = norm_g.astype(F32).reshape(HG_HEADS, 1, d)
    e_np, m_np = _hg_tables()
    e_all = jnp.asarray(np.concatenate([e_np, e_np, e_np], axis=1), BF16)
    masks = jnp.asarray(m_np, F32)
    return pl.pallas_call(
        functools.partial(_hgrn2_kernel, layer, rows),
        grid=(batch, nst, pairs),
        in_specs=[
            pl.BlockSpec((rows, D_MODEL), lambda b, s, p: (b * nst + s, 0)),
            *head_cols(lambda p: 0),
            *head_cols(lambda p: 2 * p + 1),
            *head_cols(lambda p: jnp.minimum(2 * p + 2, HG_HEADS - 1)),
            pl.BlockSpec((2, DEPTH + 1, d), lambda b, s, p: (p, 0, 0)),
            pl.BlockSpec((2, 1, d), lambda b, s, p: (p, 0, 0)),
            pl.BlockSpec(e_all.shape, lambda b, s, p: (0, 0)),
            pl.BlockSpec(masks.shape, lambda b, s, p: (0, 0, 0)),
        ],
        out_specs=pl.BlockSpec((rows, 2 * d), lambda b, s, p: (b * nst + s, p)),
        out_shape=jax.ShapeDtypeStruct((m, HG_HEADS * d), BF16),
        scratch_shapes=[
            pltpu.VMEM((rows, D_MODEL), BF16),
            pltpu.VMEM((rows, 4 * d), F32),
            pltpu.VMEM((rows, 4 * d), F32),
            pltpu.VMEM((HG_HEADS, d, d), F32),
        ] + 2 * [
            pltpu.VMEM((rows, d), F32),
            pltpu.VMEM((rows, d), F32),
            pltpu.VMEM((rows // (2 * HG_CHUNK), 8 * HG_CHUNK, 2 * d), F32),
        ],
        compiler_params=_params("arbitrary", "arbitrary", "arbitrary"),
        name="hgrn2_mixer",
    )(x2d, *(12 * [w]), lbl, ng, e_all, masks)


_LN_UNROLL = 4


def _ln_rows(ln_rows, n_rows, z_of, g, b, o_ref, o16_ref=None):
    def body(r, carry):
        rs = pl.ds(pl.multiple_of(r * ln_rows, ln_rows), ln_rows)
        y = _layer_norm(z_of(rs), g, b)
        o_ref[rs, :] = y
        if o16_ref is not None:
            o16_ref[rs, :] = y.astype(BF16)
        return carry

    lax.fori_loop(0, n_rows // ln_rows, body, 0, unroll=_LN_UNROLL)


def _outproj_ln_kernel(rows, ln_rows, y0_ref, yodd_ref, ynext_ref, w_ref, x_ref, g_ref, b_ref,
                       o_ref, acca_ref, accb_ref):
    t = pl.program_id(0)

    @pl.when(t == 0)
    def _():
        acca_ref[...] = _dot(y0_ref[...], w_ref[...])

    g = g_ref[...]
    b = b_ref[...]

    def ln(acc_ref, row0):
        for r0 in range(0, rows, ln_rows):
            xs = slice(row0 + r0, row0 + r0 + ln_rows)
            z = ALPHA * x_ref[xs, :] + acc_ref[r0:r0 + ln_rows, :]
            o_ref[xs, :] = _layer_norm(z, g, b)

    accb_ref[...] = _dot(yodd_ref[...], w_ref[...])
    ln(acca_ref, 0)
    acca_ref[...] = _dot(ynext_ref[...], w_ref[...])
    ln(accb_ref, rows)


def _outproj_ln(y2d, w_out, x2d, g, b, rows, ln_rows):
    m = x2d.shape[0]
    n_sub = m // rows
    once = pl.Buffered(1)
    return pl.pallas_call(
        functools.partial(_outproj_ln_kernel, rows, ln_rows),
        grid=(n_sub // 2,),
        in_specs=[
            pl.BlockSpec((rows, D_MODEL), lambda t: (0, 0), pipeline_mode=once),
            pl.BlockSpec((rows, D_MODEL), lambda t: (2 * t + 1, 0)),
            pl.BlockSpec((rows, D_MODEL), lambda t: (jnp.minimum(2 * t + 2, n_sub - 1), 0)),
            pl.BlockSpec((D_MODEL, D_MODEL), lambda t: (0, 0), pipeline_mode=once),
            pl.BlockSpec((2 * rows, D_MODEL), lambda t: (t, 0)),
            pl.BlockSpec((1, D_MODEL), lambda t: (0, 0)),
            pl.BlockSpec((1, D_MODEL), lambda t: (0, 0)),
        ],
        out_specs=pl.BlockSpec((2 * rows, D_MODEL), lambda t: (t, 0)),
        out_shape=jax.ShapeDtypeStruct((m, D_MODEL), F32),
        scratch_shapes=[pltpu.VMEM((rows, D_MODEL), F32), pltpu.VMEM((rows, D_MODEL), F32)],
        compiler_params=_params("arbitrary"),
        name="outproj_ln",
    )(y2d, y2d, y2d, w_out.astype(BF16), x2d, g.astype(F32).reshape(1, D_MODEL),
      b.astype(F32).reshape(1, D_MODEL))


_HALO = BF16_SUBLANE_TILE
_CONV_ROWS = 64
_FFN_SUB = 512


def _ffn_kernel(rows, ln_rows, seq_tiles, with_bf16, x_ref, xp_ref, wa_ref, wb_ref, cw_ref,
                cb_ref, wd_ref, g_ref, b_ref, o_ref, *rest):
    o16_ref = rest[0] if with_bf16 else None
    xe_ref, a_ref, bb_ref, hm_ref = rest[-4:]
    i = pl.program_id(0)
    j = pl.program_id(1)

    @pl.when(j == 0)
    def _():
        keep = (i % seq_tiles) != 0
        xe_ref[0:_HALO, :] = jnp.where(keep, xp_ref[...], 0.0).astype(BF16)
        xe_ref[_HALO:, :] = x_ref[...].astype(BF16)
        o_ref[...] = jnp.zeros_like(o_ref)

    cw = cw_ref[...]
    cb = cb_ref[...]
    for s0 in range(0, rows, _FFN_SUB):
        lo = 0 if s0 == 0 else _HALO + s0
        hi = _HALO + s0 + _FFN_SUB
        a_ref[lo:hi, :] = _dot(xe_ref[lo:hi, :], wa_ref[...])
        bb_ref[s0:s0 + _FFN_SUB, :] = _dot(xe_ref[_HALO + s0:hi, :], wb_ref[...])
        for r0 in range(s0, s0 + _FFN_SUB, _CONV_ROWS):
            a_0 = a_ref[_HALO + r0:_HALO + r0 + _CONV_ROWS, :]
            a_1 = a_ref[_HALO + r0 - 1:_HALO + r0 - 1 + _CONV_ROWS, :]
            a_2 = a_ref[_HALO + r0 - 2:_HALO + r0 - 2 + _CONV_ROWS, :]
            conv = cw[0:1] * a_2 + cw[1:2] * a_1 + cw[2:3] * a_0 + cb
            hm = _silu(conv) * bb_ref[r0:r0 + _CONV_ROWS, :]
            hm_ref[r0:r0 + _CONV_ROWS, :] = hm.astype(BF16)
        o_ref[s0:s0 + _FFN_SUB, :] += _dot(hm_ref[s0:s0 + _FFN_SUB, :], wd_ref[...])

    @pl.when(j == pl.num_programs(1) - 1)
    def _():
        g = g_ref[...]
        b = b_ref[...]
        for r0 in range(0, rows, ln_rows):
            rs = slice(r0, r0 + ln_rows)
            y = _layer_norm(ALPHA * x_ref[rs, :] + o_ref[rs, :], g, b)
            o_ref[rs, :] = y
            if o16_ref is not None:
                o16_ref[rs, :] = y.astype(BF16)


def _conv_ffn_ln(x2d, seq, layer, w_up, conv_w, conv_b, w_down, g, b, rows, cols, ln_rows,
                 with_bf16):
    m = x2d.shape[0]
    nj = D_FF // cols
    halo_blocks = rows // _HALO
    out_spec = pl.BlockSpec((rows, D_MODEL), lambda i, j: (i, 0))
    out_f32 = jax.ShapeDtypeStruct((m, D_MODEL), F32)
    return pl.pallas_call(
        functools.partial(_ffn_kernel, rows, ln_rows, seq // rows, with_bf16),
        grid=(m // rows, nj),
        in_specs=[
            pl.BlockSpec((rows, D_MODEL), lambda i, j: (i, 0), pipeline_mode=pl.Buffered(1)),
            pl.BlockSpec((_HALO, D_MODEL), lambda i, j: (jnp.maximum(i * halo_blocks - 1, 0), 0)),
            pl.BlockSpec((None, D_MODEL, cols), lambda i, j: (layer, 0, j)),
            pl.BlockSpec((None, D_MODEL, cols), lambda i, j: (layer, 0, nj + j)),
            pl.BlockSpec((CONV_WIDTH, cols), lambda i, j: (0, j)),
            pl.BlockSpec((1, cols), lambda i, j: (0, j)),
            pl.BlockSpec((None, cols, D_MODEL), lambda i, j: (layer, j, 0)),
            pl.BlockSpec((1, D_MODEL), lambda i, j: (0, 0)),
            pl.BlockSpec((1, D_MODEL), lambda i, j: (0, 0)),
        ],
        out_specs=(out_spec, out_spec) if with_bf16 else out_spec,
        out_shape=((out_f32, jax.ShapeDtypeStruct((m, D_MODEL), BF16)) if with_bf16
                   else out_f32),
        scratch_shapes=[
            pltpu.VMEM((_HALO + rows, D_MODEL), BF16),
            pltpu.VMEM((_HALO + rows, cols), F32),
            pltpu.VMEM((rows, cols), F32),
            pltpu.VMEM((rows, cols), BF16),
        ],
        compiler_params=_params("arbitrary", "arbitrary"),
        name="conv_ffn_ln",
    )(x2d, x2d, w_up, w_up, conv_w.astype(F32), conv_b.astype(F32).reshape(1, D_FF),
      w_down, g.astype(F32).reshape(1, D_MODEL), b.astype(F32).reshape(1, D_MODEL))


def _gmlp_kernel(rows, ln_rows, xc_ref, xn_ref, wu_ref, wv_ref, lng_ref, lnb_ref, ws_ref,
                 bias_ref, o_ref, zv_ref, zu_ref, vn_ref, gate_ref, wc_ref):
    i = pl.program_id(0)
    c = SG_CHUNK
    d = SG_DIM

    @pl.when(i == 0)
    def _():
        causal = (lax.broadcasted_iota(jnp.int32, (c, c), 0)
                  >= lax.broadcasted_iota(jnp.int32, (c, c), 1))
        for g in range(SG_GROUPS):
            wc_ref[g] = jnp.where(causal, ws_ref[g], 0.0).astype(BF16)
        zv_ref[...] = _dot(xc_ref[...], wv_ref[...])

    zu_ref[...] = _dot(xc_ref[...], wu_ref[...])
    lng = lng_ref[...]
    lnb = lnb_ref[...]
    for r0 in range(0, rows, ln_rows):
        rs = slice(r0, r0 + ln_rows)
        vn_ref[rs, :] = _layer_norm(_gelu_exact(zv_ref[rs, :]), lng, lnb).astype(BF16)
    for r0 in range(0, rows, c):
        rs = slice(r0, r0 + c)
        for g in range(SG_GROUPS):
            gs = slice(g * d, (g + 1) * d)
            gate_ref[rs, gs] = _dot(wc_ref[g], vn_ref[rs, gs]) + bias_ref[:, gs]
    zv_ref[...] = _dot(xn_ref[...], wv_ref[...])
    for r0 in range(0, rows, ln_rows):
        rs = slice(r0, r0 + ln_rows)
        o_ref[rs, :] = (_gelu_exact(zu_ref[rs, :]) * gate_ref[rs, :]).astype(BF16)


def _gmlp_mixer(x16, w_in, ln_g, ln_b, w_s, b_s, rows, ln_rows):
    m = x16.shape[0]
    n_tiles = m // rows
    width = SG_GROUPS * SG_DIM
    bias = jnp.broadcast_to(b_s.astype(F32).T[:, :, None], (SG_CHUNK, SG_GROUPS, SG_DIM))
    bias = bias.reshape(SG_CHUNK, width)
    w_in = w_in.astype(BF16)
    once = pl.Buffered(1)
    return pl.pallas_call(
        functools.partial(_gmlp_kernel, rows, ln_rows),
        grid=(n_tiles,),
        in_specs=[
            pl.BlockSpec((rows, D_MODEL), lambda i: (i, 0)),
            pl.BlockSpec((rows, D_MODEL), lambda i: (jnp.minimum(i + 1, n_tiles - 1), 0)),
            pl.BlockSpec((D_MODEL, width), lambda i: (0, 0), pipeline_mode=once),
            pl.BlockSpec((D_MODEL, width), lambda i: (0, 1), pipeline_mode=once),
            pl.BlockSpec((1, width), lambda i: (0, 0)),
            pl.BlockSpec((1, width), lambda i: (0, 0)),
            pl.BlockSpec((SG_GROUPS, SG_CHUNK, SG_CHUNK), lambda i: (0, 0, 0), pipeline_mode=once),
            pl.BlockSpec((SG_CHUNK, width), lambda i: (0, 0), pipeline_mode=once),
        ],
        out_specs=pl.BlockSpec((rows, width), lambda i: (i, 0)),
        out_shape=jax.ShapeDtypeStruct((m, width), BF16),
        scratch_shapes=[
            pltpu.VMEM((rows, width), F32),
            pltpu.VMEM((rows, width), F32),
            pltpu.VMEM((rows, width), BF16),
            pltpu.VMEM((rows, width), F32),
            pltpu.VMEM((SG_GROUPS, SG_CHUNK, SG_CHUNK), BF16),
        ],
        compiler_params=_params("arbitrary"),
        name="gmlp_mixer",
    )(x16, x16, w_in, w_in, ln_g.astype(F32).reshape(1, width),
      ln_b.astype(F32).reshape(1, width), w_s.astype(F32), bias)


def kernel(x, lb_logits, hg_w_in, hg_norm_g, hg_w_out, sg_w_in, sg_ln_g, sg_ln_b, sg_w_s,
           sg_b_s, sg_w_out, ffn_w_up, ffn_conv_w, ffn_conv_b, ffn_w_down, ln1_g, ln1_b,
           ln2_g, ln2_b):
    batch, seq, _ = x.shape
    t = _tiles(batch, seq)
    h = x.astype(F32).reshape(batch * seq, D_MODEL)
    h16 = None
    w_up16 = ffn_w_up.astype(BF16)
    w_down16 = ffn_w_down.astype(BF16)
    for layer in range(DEPTH):
        occ = layer // N_MIXERS
        if layer % N_MIXERS == 0:
            mixed = _hgrn2_mixer(h, batch, seq, layer, lb_logits, hg_w_in[occ], hg_norm_g[occ],
                                 t["hg_rows"])
            w_out = hg_w_out[occ]
        else:
            mixed = _gmlp_mixer(h16 if h16 is not None else h.astype(BF16), sg_w_in[occ],
                                sg_ln_g[occ], sg_ln_b[occ], sg_w_s[occ], sg_b_s[occ],
                                t["sg_rows"], t["ln_rows"])
            w_out = sg_w_out[occ]
        h = _outproj_ln(mixed, w_out, h, ln1_g[layer], ln1_b[layer], t["out_rows"], t["ln_rows"])
        next_is_gmlp = layer + 1 < DEPTH and (layer + 1) % N_MIXERS == 1
        res = _conv_ffn_ln(h, seq, layer, w_up16, ffn_conv_w[layer], ffn_conv_b[layer],
                           w_down16, ln2_g[layer], ln2_b[layer],
                           t["ffn_rows"], t["ffn_cols"], t["ln_rows"], next_is_gmlp)
        h, h16 = res if next_is_gmlp else (res, None)
    return h.reshape(batch, seq, D_MODEL).astype(x.dtype)
```

```python
import functools

import numpy as np
import jax
import jax.numpy as jnp
from jax import lax
from jax.experimental import pallas as pl
from jax.experimental.pallas import tpu as pltpu

D_MODEL = 2048
DEPTH = 2
N_MIXERS = 2
HG_HEADS = 16
HG_DIM = 128
HG_CHUNK = 64
SG_GROUPS = 16
SG_DIM = 128
SG_CHUNK = 128
D_FF = 5632
CONV_WIDTH = 3
ALPHA = (2 * DEPTH) ** 0.25
LN_EPS = 1e-5
RMS_EPS = 1e-6

F32 = jnp.float32
BF16 = jnp.bfloat16

V7X_VMEM_LIMIT_BYTES = 56 * 1024 * 1024
BF16_SUBLANE_TILE = 16


def _tiles(batch, seq):
    rows = batch * seq
    t = 512
    assert seq % t == 0 and rows % t == 0 and D_FF % 512 == 0
    return dict(hg_rows=2 * t, out_rows=256, ffn_rows=2 * t, ffn_cols=512, sg_rows=t, ln_rows=32)


def _sigmoid_pair(z):
    e = jnp.exp(-jnp.abs(z))
    r = 1.0 / (1.0 + e)
    er = e * r
    pos = z >= 0
    return jnp.where(pos, r, er), jnp.where(pos, er, r)


def _silu(z):
    return z / (1.0 + jnp.exp(-z))


def _gelu_exact(z):
    return 0.5 * z * (1.0 + lax.erf(z * (2.0 ** -0.5)))


def _layer_norm(z, g, b):
    mu = jnp.mean(z, axis=-1, keepdims=True)
    zc = z - mu
    var = jnp.mean(zc * zc, axis=-1, keepdims=True)
    return zc * lax.rsqrt(var + LN_EPS) * g + b


def _dot(a, b):
    return jnp.dot(a, b, preferred_element_type=F32)


def _dot_nt(a, b):
    return lax.dot_general(a, b, (((1,), (1,)), ((), ())), preferred_element_type=F32)


def _dot_tn(a, b):
    return lax.dot_general(a, b, (((0,), (0,)), ((), ())), preferred_element_type=F32)


def _params(*semantics):
    return pltpu.CompilerParams(dimension_semantics=semantics,
                                vmem_limit_bytes=V7X_VMEM_LIMIT_BYTES)


_HG_LEVELS = (32, 16, 8, 4, 2, 1)
_N_LEV = len(_HG_LEVELS)


def _hg_tables():
    c = HG_CHUNK
    e = np.zeros((8 * c, c), np.float32)
    masks = np.zeros((_N_LEV + 1, c, c), np.float32)
    t = np.arange(c)
    for li, h in enumerate(_HG_LEVELS):
        for tt in range(c):
            pos = tt % (2 * h)
            m = tt - pos + h
            if pos >= h:
                e[li * c + tt, m:tt + 1] = 1.0
            else:
                e[li * c + tt, tt + 1:m] = 1.0
        blk = t // (2 * h)
        upper = (t % (2 * h)) >= h
        masks[li] = (blk[:, None] == blk[None, :]) & upper[:, None] & (~upper)[None, :]
    masks[_N_LEV] = np.eye(c)
    e[6 * c:7 * c] = np.tril(np.ones((c, c)))
    e[7 * c:8 * c] = np.triu(np.ones((c, c)), 1)
    return e, masks


def _hg_head(layer, rows, proj_ref, lbl, ng, e3, m_ref, st, o_ref, lane0, q_ref, k_ref,
             xs_ref, fill):
    c = HG_CHUNK
    d = HG_DIM
    n_chunks = rows // c
    lrows = [lbl[i:i + 1, :] for i in range(DEPTH + 1)]
    mx = functools.reduce(jnp.maximum, lrows)
    ex = [jnp.exp(r - mx) for r in lrows]
    lb = sum(ex[:layer + 1]) / sum(ex)
    one_m_lb = 1.0 - lb

    def rs(ci):
        return slice(ci * c, (ci + 1) * c)

    def xs(ci, block):
        return xs_ref[ci // 2, block * c:(block + 1) * c, (ci % 2) * d:(ci % 2 + 1) * d]

    fill = iter(fill)

    def emit_fill():
        next(fill)()

    emit_fill()
    for pair in range(n_chunks // 2):
        lfs = []
        for ci in (2 * pair, 2 * pair + 1):
            q_ref[rs(ci), :] = _silu(proj_ref[rs(ci), 0:d])
            sig, sig_neg = _sigmoid_pair(proj_ref[rs(ci), d:2 * d])
            lfs.append(jnp.log(lb + one_m_lb * sig))
            k_ref[rs(ci), :] = one_m_lb * sig_neg
        lf2 = jnp.concatenate(lfs, axis=1)
        hi = lf2.astype(BF16)
        r1 = lf2 - hi.astype(F32)
        mid = r1.astype(BF16)
        lo = (r1 - mid.astype(F32)).astype(BF16)
        xs_ref[pair] = _dot(e3, jnp.concatenate([hi, mid, lo], axis=0))
    for _ in range(_hg_n_fill(rows) - 1):
        emit_fill()
    scores = []
    for ci in range(n_chunks):
        q = q_ref[rs(ci), :].astype(BF16)
        k = k_ref[rs(ci), :].astype(BF16)
        sc = _dot_nt(q, k) * m_ref[_N_LEV]
        for li in range(_N_LEV):
            w = jnp.exp(xs(ci, li)).astype(BF16)
            sc = sc + _dot_nt(q * w, k * w) * m_ref[li]
        scores.append(sc.astype(BF16))
    intra, kvs, qhs, decays = [], [], [], []
    for ci in range(n_chunks):
        v_b = proj_ref[rs(ci), 2 * d:3 * d].astype(BF16)
        cum = xs(ci, 6)
        kvs.append(_dot_tn(v_b, (k_ref[rs(ci), :] * jnp.exp(xs(ci, 7))).astype(BF16)))
        intra.append(_dot(scores[ci], v_b))
        qhs.append((q_ref[rs(ci), :] * jnp.exp(cum)).astype(BF16))
        decays.append(jnp.exp(cum[c - 1:c, :]))
    outs = []
    for ci in range(n_chunks):
        outs.append(intra[ci] + _dot_nt(qhs[ci], st.astype(BF16)))
        st = st * decays[ci] + kvs[ci]
    for ci in range(n_chunks):
        out = outs[ci]
        out = out * lax.rsqrt(jnp.mean(out * out, axis=-1, keepdims=True) + RMS_EPS)
        o_ref[rs(ci), lane0:lane0 + d] = (
            (out * ng * _silu(proj_ref[rs(ci), 3 * d:4 * d])).astype(BF16))
    return st


def _hg_n_fill(rows):
    return 4


def _proj_pieces(xb_ref, w_refs, proj_ref, n_pieces):
    kdim, d = w_refs[0].shape
    n_split = 2
    k_split = n_pieces // n_split
    kw = kdim // k_split
    pieces = []
    for ni in range(n_split):
        ns = slice(ni * 2 * d, (ni + 1) * 2 * d)
        for ki in range(k_split):
            ks = slice(ki * kw, (ki + 1) * kw)

            def piece(ni=ni, ns=ns, ks=ks, first=(ki == 0)):
                w = jnp.concatenate([w_refs[2 * ni][ks, :], w_refs[2 * ni + 1][ks, :]], axis=1)
                part = _dot(xb_ref[:, ks], w)
                if first:
                    proj_ref[:, ns] = part
                else:
                    proj_ref[:, ns] += part

            pieces.append(piece)
    return pieces


def _hgrn2_kernel(layer, rows, x_ref, *refs):
    w0_refs, wodd_refs, wnext_refs = refs[0:4], refs[4:8], refs[8:12]
    (lbl_ref, ng_ref, e_ref, m_ref, o_ref, xb_ref, pa_ref, pb_ref, st_ref, qa_ref, ka_ref,
     xa_ref, qb_ref, kb_ref, xsb_ref) = refs[12:]
    s = pl.program_id(1)
    p = pl.program_id(2)
    d = HG_DIM

    @pl.when(p == 0)
    def _():
        xb_ref[...] = x_ref[...].astype(BF16)
        pa_ref[...] = _dot(xb_ref[...], jnp.concatenate([r[...] for r in w0_refs], axis=1))

    @pl.when(s == 0)
    def _():
        st_ref[2 * p] = jnp.zeros((d, d), F32)
        st_ref[2 * p + 1] = jnp.zeros((d, d), F32)

    e3 = e_ref[...]
    st_ref[2 * p] = _hg_head(layer, rows, pa_ref, lbl_ref[0], ng_ref[0], e3, m_ref,
                             st_ref[2 * p], o_ref, 0, qa_ref, ka_ref, xa_ref,
                             _proj_pieces(xb_ref, wodd_refs, pb_ref, _hg_n_fill(rows)))
    st_ref[2 * p + 1] = _hg_head(layer, rows, pb_ref, lbl_ref[1], ng_ref[1], e3, m_ref,
                                 st_ref[2 * p + 1], o_ref, d, qb_ref, kb_ref, xsb_ref,
                                 _proj_pieces(xb_ref, wnext_refs, pa_ref, _hg_n_fill(rows)))


def _hgrn2_mixer(x2d, batch, seq, layer, lb_logits, w_in, norm_g, rows):
    m = x2d.shape[0]
    nst = seq // rows
    d = HG_DIM
    pairs = HG_HEADS // 2
    w = w_in.astype(BF16)

    def head_cols(head_of):
        return [pl.BlockSpec((D_MODEL, d),
                             lambda b, s, p, sec=sec: (0, sec * HG_HEADS + head_of(p)))
                for sec in range(4)]

    lbl = lb_logits.astype(F32).reshape(DEPTH + 1, HG_HEADS, d).transpose(1, 0, 2)
    ng = norm_g.astype(F32).reshape(HG_HEADS, 1, d)
    e_np, m_np = _hg_tables()
    e_all = jnp.asarray(np.concatenate([e_np, e_np, e_np], axis=1), BF16)
    masks = jnp.asarray(m_np, F32)
    return pl.pallas_call(
        functools.partial(_hgrn2_kernel, layer, rows),
        grid=(batch, nst, pairs),
        in_specs=[
            pl.BlockSpec((rows, D_MODEL), lambda b, s, p: (b * nst + s, 0)),
            *head_cols(lambda p: 0),
            *head_cols(lambda p: 2 * p + 1),
            *head_cols(lambda p: jnp.minimum(2 * p + 2, HG_HEADS - 1)),
            pl.BlockSpec((2, DEPTH + 1, d), lambda b, s, p: (p, 0, 0)),
            pl.BlockSpec((2, 1, d), lambda b, s, p: (p, 0, 0)),
            pl.BlockSpec(e_all.shape, lambda b, s, p: (0, 0)),
            pl.BlockSpec(masks.shape, lambda b, s, p: (0, 0, 0)),
        ],
        out_specs=pl.BlockSpec((rows, 2 * d), lambda b, s, p: (b * nst + s, p)),
        out_shape=jax.ShapeDtypeStruct((m, HG_HEADS * d), BF16),
        scratch_shapes=[
            pltpu.VMEM((rows, D_MODEL), BF16),
            pltpu.VMEM((rows, 4 * d), F32),
            pltpu.VMEM((rows, 4 * d), F32),
            pltpu.VMEM((HG_HEADS, d, d), F32),
        ] + 2 * [
            pltpu.VMEM((rows, d), F32),
            pltpu.VMEM((rows, d), F32),
            pltpu.VMEM((rows // (2 * HG_CHUNK), 8 * HG_CHUNK, 2 * d), F32),
        ],
        compiler_params=_params("arbitrary", "arbitrary", "arbitrary"),
        name="hgrn2_mixer",
    )(x2d, *(12 * [w]), lbl, ng, e_all, masks)


_LN_UNROLL = 4


def _ln_rows(ln_rows, n_rows, z_of, g, b, o_ref, o16_ref=None):
    def body(r, carry):
        rs = pl.ds(pl.multiple_of(r * ln_rows, ln_rows), ln_rows)
        y = _layer_norm(z_of(rs), g, b)
        o_ref[rs, :] = y
        if o16_ref is not None:
            o16_ref[rs, :] = y.astype(BF16)
        return carry

    lax.fori_loop(0, n_rows // ln_rows, body, 0, unroll=_LN_UNROLL)


def _outproj_ln_kernel(rows, ln_rows, y0_ref, yodd_ref, ynext_ref, w_ref, x_ref, g_ref, b_ref,
                       o_ref, acca_ref, accb_ref):
    t = pl.program_id(0)

    @pl.when(t == 0)
    def _():
        acca_ref[...] = _dot(y0_ref[...], w_ref[...])

    g = g_ref[...]
    b = b_ref[...]

    def ln(acc_ref, row0):
        for r0 in range(0, rows, ln_rows):
            xs = slice(row0 + r0, row0 + r0 + ln_rows)
            z = ALPHA * x_ref[xs, :] + acc_ref[r0:r0 + ln_rows, :]
            o_ref[xs, :] = _layer_norm(z, g, b)

    accb_ref[...] = _dot(yodd_ref[...], w_ref[...])
    ln(acca_ref, 0)
    acca_ref[...] = _dot(ynext_ref[...], w_ref[...])
    ln(accb_ref, rows)


def _outproj_ln(y2d, w_out, x2d, g, b, rows, ln_rows):
    m = x2d.shape[0]
    n_sub = m // rows
    once = pl.Buffered(1)
    return pl.pallas_call(
        functools.partial(_outproj_ln_kernel, rows, ln_rows),
        grid=(n_sub // 2,),
        in_specs=[
            pl.BlockSpec((rows, D_MODEL), lambda t: (0, 0), pipeline_mode=once),
            pl.BlockSpec((rows, D_MODEL), lambda t: (2 * t + 1, 0)),
            pl.BlockSpec((rows, D_MODEL), lambda t: (jnp.minimum(2 * t + 2, n_sub - 1), 0)),
            pl.BlockSpec((D_MODEL, D_MODEL), lambda t: (0, 0), pipeline_mode=once),
            pl.BlockSpec((2 * rows, D_MODEL), lambda t: (t, 0)),
            pl.BlockSpec((1, D_MODEL), lambda t: (0, 0)),
            pl.BlockSpec((1, D_MODEL), lambda t: (0, 0)),
        ],
        out_specs=pl.BlockSpec((2 * rows, D_MODEL), lambda t: (t, 0)),
        out_shape=jax.ShapeDtypeStruct((m, D_MODEL), F32),
        scratch_shapes=[pltpu.VMEM((rows, D_MODEL), F32), pltpu.VMEM((rows, D_MODEL), F32)],
        compiler_params=_params("arbitrary"),
        name="outproj_ln",
    )(y2d, y2d, y2d, w_out.astype(BF16), x2d, g.astype(F32).reshape(1, D_MODEL),
      b.astype(F32).reshape(1, D_MODEL))


_HALO = BF16_SUBLANE_TILE
_CONV_ROWS = 64


def _ffn_kernel(rows, ln_rows, seq_tiles, with_bf16, x_ref, xp_ref, wa_ref, wb_ref, cw_ref,
                cb_ref, wd_ref, g_ref, b_ref, o_ref, *rest):
    o16_ref = rest[0] if with_bf16 else None
    xe_ref, a_ref, bb_ref, hm_ref = rest[-4:]
    i = pl.program_id(0)
    j = pl.program_id(1)

    @pl.when(j == 0)
    def _():
        keep = (i % seq_tiles) != 0
        xe_ref[0:_HALO, :] = jnp.where(keep, xp_ref[...], 0.0).astype(BF16)
        xe_ref[_HALO:, :] = x_ref[...].astype(BF16)
        o_ref[...] = jnp.zeros_like(o_ref)

    cw = cw_ref[...]
    cb = cb_ref[...]
    half = a_ref.shape[1] // 2
    for cs in (slice(0, half), slice(half, 2 * half)):
        a_ref[:, cs] = _dot(xe_ref[...], wa_ref[:, cs])
        bb_ref[:, cs] = _dot(xe_ref[_HALO:, :], wb_ref[:, cs])
        for r0 in range(0, rows, _CONV_ROWS):
            a_0 = a_ref[_HALO + r0:_HALO + r0 + _CONV_ROWS, cs]
            a_1 = a_ref[_HALO + r0 - 1:_HALO + r0 - 1 + _CONV_ROWS, cs]
            a_2 = a_ref[_HALO + r0 - 2:_HALO + r0 - 2 + _CONV_ROWS, cs]
            conv = cw[0:1, cs] * a_2 + cw[1:2, cs] * a_1 + cw[2:3, cs] * a_0 + cb[:, cs]
            hm = _silu(conv) * bb_ref[r0:r0 + _CONV_ROWS, cs]
            hm_ref[r0:r0 + _CONV_ROWS, cs] = hm.astype(BF16)
    o_ref[...] += _dot(hm_ref[...], wd_ref[...])

    @pl.when(j == pl.num_programs(1) - 1)
    def _():
        g = g_ref[...]
        b = b_ref[...]
        for r0 in range(0, rows, ln_rows):
            rs = slice(r0, r0 + ln_rows)
            y = _layer_norm(ALPHA * x_ref[rs, :] + o_ref[rs, :], g, b)
            o_ref[rs, :] = y
            if o16_ref is not None:
                o16_ref[rs, :] = y.astype(BF16)


def _conv_ffn_ln(x2d, seq, layer, w_up, conv_w, conv_b, w_down, g, b, rows, cols, ln_rows,
                 with_bf16):
    m = x2d.shape[0]
    nj = D_FF // cols
    halo_blocks = rows // _HALO
    out_spec = pl.BlockSpec((rows, D_MODEL), lambda i, j: (i, 0))
    out_f32 = jax.ShapeDtypeStruct((m, D_MODEL), F32)
    return pl.pallas_call(
        functools.partial(_ffn_kernel, rows, ln_rows, seq // rows, with_bf16),
        grid=(m // rows, nj),
        in_specs=[
            pl.BlockSpec((rows, D_MODEL), lambda i, j: (i, 0), pipeline_mode=pl.Buffered(1)),
            pl.BlockSpec((_HALO, D_MODEL), lambda i, j: (jnp.maximum(i * halo_blocks - 1, 0), 0)),
            pl.BlockSpec((None, D_MODEL, cols), lambda i, j: (layer, 0, j)),
            pl.BlockSpec((None, D_MODEL, cols), lambda i, j: (layer, 0, nj + j)),
            pl.BlockSpec((CONV_WIDTH, cols), lambda i, j: (0, j)),
            pl.BlockSpec((1, cols), lambda i, j: (0, j)),
            pl.BlockSpec((None, cols, D_MODEL), lambda i, j: (layer, j, 0)),
            pl.BlockSpec((1, D_MODEL), lambda i, j: (0, 0)),
            pl.BlockSpec((1, D_MODEL), lambda i, j: (0, 0)),
        ],
        out_specs=(out_spec, out_spec) if with_bf16 else out_spec,
        out_shape=((out_f32, jax.ShapeDtypeStruct((m, D_MODEL), BF16)) if with_bf16
                   else out_f32),
        scratch_shapes=[
            pltpu.VMEM((_HALO + rows, D_MODEL), BF16),
            pltpu.VMEM((_HALO + rows, cols), F32),
            pltpu.VMEM((rows, cols), F32),
            pltpu.VMEM((rows, cols), BF16),
        ],
        compiler_params=_params("arbitrary", "arbitrary"),
        name="conv_ffn_ln",
    )(x2d, x2d, w_up, w_up, conv_w.astype(F32), conv_b.astype(F32).reshape(1, D_FF),
      w_down, g.astype(F32).reshape(1, D_MODEL), b.astype(F32).reshape(1, D_MODEL))


def _gmlp_kernel(rows, ln_rows, xc_ref, xn_ref, wu_ref, wv_ref, lng_ref, lnb_ref, ws_ref,
                 bias_ref, o_ref, zv_ref, zu_ref, vn_ref, gate_ref, wc_ref):
    i = pl.program_id(0)
    c = SG_CHUNK
    d = SG_DIM

    @pl.when(i == 0)
    def _():
        causal = (lax.broadcasted_iota(jnp.int32, (c, c), 0)
                  >= lax.broadcasted_iota(jnp.int32, (c, c), 1))
        for g in range(SG_GROUPS):
            wc_ref[g] = jnp.where(causal, ws_ref[g], 0.0).astype(BF16)
        zv_ref[...] = _dot(xc_ref[...], wv_ref[...])

    zu_ref[...] = _dot(xc_ref[...], wu_ref[...])
    lng = lng_ref[...]
    lnb = lnb_ref[...]
    for r0 in range(0, rows, ln_rows):
        rs = slice(r0, r0 + ln_rows)
        vn_ref[rs, :] = _layer_norm(_gelu_exact(zv_ref[rs, :]), lng, lnb).astype(BF16)
    for r0 in range(0, rows, c):
        rs = slice(r0, r0 + c)
        for g in range(SG_GROUPS):
            gs = slice(g * d, (g + 1) * d)
            gate_ref[rs, gs] = _dot(wc_ref[g], vn_ref[rs, gs]) + bias_ref[:, gs]
    zv_ref[...] = _dot(xn_ref[...], wv_ref[...])
    for r0 in range(0, rows, ln_rows):
        rs = slice(r0, r0 + ln_rows)
        o_ref[rs, :] = (_gelu_exact(zu_ref[rs, :]) * gate_ref[rs, :]).astype(BF16)


def _gmlp_mixer(x16, w_in, ln_g, ln_b, w_s, b_s, rows, ln_rows):
    m = x16.shape[0]
    n_tiles = m // rows
    width = SG_GROUPS * SG_DIM
    bias = jnp.broadcast_to(b_s.astype(F32).T[:, :, None], (SG_CHUNK, SG_GROUPS, SG_DIM))
    bias = bias.reshape(SG_CHUNK, width)
    w_in = w_in.astype(BF16)
    once = pl.Buffered(1)
    return pl.pallas_call(
        functools.partial(_gmlp_kernel, rows, ln_rows),
        grid=(n_tiles,),
        in_specs=[
            pl.BlockSpec((rows, D_MODEL), lambda i: (i, 0)),
            pl.BlockSpec((rows, D_MODEL), lambda i: (jnp.minimum(i + 1, n_tiles - 1), 0)),
            pl.BlockSpec((D_MODEL, width), lambda i: (0, 0), pipeline_mode=once),
            pl.BlockSpec((D_MODEL, width), lambda i: (0, 1), pipeline_mode=once),
            pl.BlockSpec((1, width), lambda i: (0, 0)),
            pl.BlockSpec((1, width), lambda i: (0, 0)),
            pl.BlockSpec((SG_GROUPS, SG_CHUNK, SG_CHUNK), lambda i: (0, 0, 0), pipeline_mode=once),
            pl.BlockSpec((SG_CHUNK, width), lambda i: (0, 0), pipeline_mode=once),
        ],
        out_specs=pl.BlockSpec((rows, width), lambda i: (i, 0)),
        out_shape=jax.ShapeDtypeStruct((m, width), BF16),
        scratch_shapes=[
            pltpu.VMEM((rows, width), F32),
            pltpu.VMEM((rows, width), F32),
            pltpu.VMEM((rows, width), BF16),
            pltpu.VMEM((rows, width), F32),
            pltpu.VMEM((SG_GROUPS, SG_CHUNK, SG_CHUNK), BF16),
        ],
        compiler_params=_params("arbitrary"),
        name="gmlp_mixer",
    )(x16, x16, w_in, w_in, ln_g.astype(F32).reshape(1, width),
      ln_b.astype(F32).reshape(1, width), w_s.astype(F32), bias)


def kernel(x, lb_logits, hg_w_in, hg_norm_g, hg_w_out, sg_w_in, sg_ln_g, sg_ln_b, sg_w_s,
           sg_b_s, sg_w_out, ffn_w_up, ffn_conv_w, ffn_conv_b, ffn_w_down, ln1_g, ln1_b,
           ln2_g, ln2_b):
    batch, seq, _ = x.shape
    t = _tiles(batch, seq)
    h = x.astype(F32).reshape(batch * seq, D_MODEL)
    h16 = None
    w_up16 = ffn_w_up.astype(BF16)
    w_down16 = ffn_w_down.astype(BF16)
    for layer in range(DEPTH):
        occ = layer // N_MIXERS
        if layer % N_MIXERS == 0:
            mixed = _hgrn2_mixer(h, batch, seq, layer, lb_logits, hg_w_in[occ], hg_norm_g[occ],
                                 t["hg_rows"])
            w_out = hg_w_out[occ]
        else:
            mixed = _gmlp_mixer(h16 if h16 is not None else h.astype(BF16), sg_w_in[occ],
                                sg_ln_g[occ], sg_ln_b[occ], sg_w_s[occ], sg_b_s[occ],
                                t["sg_rows"], t["ln_rows"])
            w_out = sg_w_out[occ]
        h = _outproj_ln(mixed, w_out, h, ln1_g[layer], ln1_b[layer], t["out_rows"], t["ln_rows"])
        next_is_gmlp = layer + 1 < DEPTH and (layer + 1) % N_MIXERS == 1
        res = _conv_ffn_ln(h, seq, layer, w_up16, ffn_conv_w[layer], ffn_conv_b[layer],
                           w_down16, ln2_g[layer], ln2_b[layer],
                           t["ffn_rows"], t["ffn_cols"], t["ln_rows"], next_is_gmlp)
        h, h16 = res if next_is_gmlp else (res, None)
    return h.reshape(batch, seq, D_MODEL).astype(x.dtype)
```

```python
import functools

import numpy as np
import jax
import jax.numpy as jnp
from jax import lax
from jax.experimental import pallas as pl
from jax.experimental.pallas import tpu as pltpu

D_MODEL = 2048
DEPTH = 2
N_MIXERS = 2
HG_HEADS = 16
HG_DIM = 128
HG_CHUNK = 64
SG_GROUPS = 16
SG_DIM = 128
SG_CHUNK = 128
D_FF = 5632
CONV_WIDTH = 3
ALPHA = (2 * DEPTH) ** 0.25
LN_EPS = 1e-5
RMS_EPS = 1e-6

F32 = jnp.float32
BF16 = jnp.bfloat16

V7X_VMEM_LIMIT_BYTES = 56 * 1024 * 1024
BF16_SUBLANE_TILE = 16


def _tiles(batch, seq):
    rows = batch * seq
    t = 512
    assert seq % t == 0 and rows % t == 0 and D_FF % 512 == 0
    return dict(hg_rows=2 * t, out_rows=256, ffn_rows=t, ffn_cols=512, sg_rows=t, ln_rows=32)


def _sigmoid_pair(z):
    e = jnp.exp(-jnp.abs(z))
    r = 1.0 / (1.0 + e)
    er = e * r
    pos = z >= 0
    return jnp.where(pos, r, er), jnp.where(pos, er, r)


def _silu(z):
    return z / (1.0 + jnp.exp(-z))


def _gelu_exact(z):
    return 0.5 * z * (1.0 + lax.erf(z * (2.0 ** -0.5)))


def _layer_norm(z, g, b):
    mu = jnp.mean(z, axis=-1, keepdims=True)
    zc = z - mu
    var = jnp.mean(zc * zc, axis=-1, keepdims=True)
    return zc * lax.rsqrt(var + LN_EPS) * g + b


def _dot(a, b):
    return jnp.dot(a, b, preferred_element_type=F32)


def _dot_nt(a, b):
    return lax.dot_general(a, b, (((1,), (1,)), ((), ())), preferred_element_type=F32)


def _dot_tn(a, b):
    return lax.dot_general(a, b, (((0,), (0,)), ((), ())), preferred_element_type=F32)


def _params(*semantics):
    return pltpu.CompilerParams(dimension_semantics=semantics,
                                vmem_limit_bytes=V7X_VMEM_LIMIT_BYTES)


_HG_LEVELS = (32, 16, 8, 4, 2, 1)
_N_LEV = len(_HG_LEVELS)


def _hg_tables():
    c = HG_CHUNK
    e = np.zeros((8 * c, c), np.float32)
    masks = np.zeros((_N_LEV + 1, c, c), np.float32)
    t = np.arange(c)
    for li, h in enumerate(_HG_LEVELS):
        for tt in range(c):
            pos = tt % (2 * h)
            m = tt - pos + h
            if pos >= h:
                e[li * c + tt, m:tt + 1] = 1.0
            else:
                e[li * c + tt, tt + 1:m] = 1.0
        blk = t // (2 * h)
        upper = (t % (2 * h)) >= h
        masks[li] = (blk[:, None] == blk[None, :]) & upper[:, None] & (~upper)[None, :]
    masks[_N_LEV] = np.eye(c)
    e[6 * c:7 * c] = np.tril(np.ones((c, c)))
    e[7 * c:8 * c] = np.triu(np.ones((c, c)), 1)
    return e, masks


def _hg_head(layer, rows, proj_ref, lbl, ng, e3, m_ref, st, o_ref, lane0, q_ref, k_ref,
             xs_ref, fill):
    c = HG_CHUNK
    d = HG_DIM
    n_chunks = rows // c
    lrows = [lbl[i:i + 1, :] for i in range(DEPTH + 1)]
    mx = functools.reduce(jnp.maximum, lrows)
    ex = [jnp.exp(r - mx) for r in lrows]
    lb = sum(ex[:layer + 1]) / sum(ex)
    one_m_lb = 1.0 - lb

    def rs(ci):
        return slice(ci * c, (ci + 1) * c)

    def xs(ci, block):
        return xs_ref[ci // 2, block * c:(block + 1) * c, (ci % 2) * d:(ci % 2 + 1) * d]

    fill = iter(fill)

    def emit_fill():
        next(fill)()

    emit_fill()
    for pair in range(n_chunks // 2):
        lfs = []
        for ci in (2 * pair, 2 * pair + 1):
            q_ref[rs(ci), :] = _silu(proj_ref[rs(ci), 0:d])
            sig, sig_neg = _sigmoid_pair(proj_ref[rs(ci), d:2 * d])
            lfs.append(jnp.log(lb + one_m_lb * sig))
            k_ref[rs(ci), :] = one_m_lb * sig_neg
        lf2 = jnp.concatenate(lfs, axis=1)
        hi = lf2.astype(BF16)
        r1 = lf2 - hi.astype(F32)
        mid = r1.astype(BF16)
        lo = (r1 - mid.astype(F32)).astype(BF16)
        xs_ref[pair] = _dot(e3, jnp.concatenate([hi, mid, lo], axis=0))
    for _ in range(_hg_n_fill(rows) - 1):
        emit_fill()
    scores = []
    for ci in range(n_chunks):
        q = q_ref[rs(ci), :].astype(BF16)
        k = k_ref[rs(ci), :].astype(BF16)
        sc = _dot_nt(q, k) * m_ref[_N_LEV]
        for li in range(_N_LEV):
            w = jnp.exp(xs(ci, li)).astype(BF16)
            sc = sc + _dot_nt(q * w, k * w) * m_ref[li]
        scores.append(sc.astype(BF16))
    intra, kvs, qhs, decays = [], [], [], []
    for ci in range(n_chunks):
        v_b = proj_ref[rs(ci), 2 * d:3 * d].astype(BF16)
        cum = xs(ci, 6)
        kvs.append(_dot_tn(v_b, (k_ref[rs(ci), :] * jnp.exp(xs(ci, 7))).astype(BF16)))
        intra.append(_dot(scores[ci], v_b))
        qhs.append((q_ref[rs(ci), :] * jnp.exp(cum)).astype(BF16))
        decays.append(jnp.exp(cum[c - 1:c, :]))
    outs = []
    for ci in range(n_chunks):
        outs.append(intra[ci] + _dot_nt(qhs[ci], st.astype(BF16)))
        st = st * decays[ci] + kvs[ci]
    for ci in range(n_chunks):
        out = outs[ci]
        out = out * lax.rsqrt(jnp.mean(out * out, axis=-1, keepdims=True) + RMS_EPS)
        o_ref[rs(ci), lane0:lane0 + d] = (
            (out * ng * _silu(proj_ref[rs(ci), 3 * d:4 * d])).astype(BF16))
    return st


def _hg_n_fill(rows):
    return 2


def _proj_pieces(xb_ref, w_refs, proj_ref, n_pieces):
    kdim, d = w_refs[0].shape
    n_split = 2
    k_split = n_pieces // n_split
    kw = kdim // k_split
    pieces = []
    for ni in range(n_split):
        ns = slice(ni * 2 * d, (ni + 1) * 2 * d)
        for ki in range(k_split):
            ks = slice(ki * kw, (ki + 1) * kw)

            def piece(ni=ni, ns=ns, ks=ks, first=(ki == 0)):
                w = jnp.concatenate([w_refs[2 * ni][ks, :], w_refs[2 * ni + 1][ks, :]], axis=1)
                part = _dot(xb_ref[:, ks], w)
                if first:
                    proj_ref[:, ns] = part
                else:
                    proj_ref[:, ns] += part

            pieces.append(piece)
    return pieces


def _hgrn2_kernel(layer, rows, x_ref, *refs):
    w0_refs, wodd_refs, wnext_refs = refs[0:4], refs[4:8], refs[8:12]
    (lbl_ref, ng_ref, e_ref, m_ref, o_ref, xb_ref, pa_ref, pb_ref, st_ref, qa_ref, ka_ref,
     xa_ref, qb_ref, kb_ref, xsb_ref) = refs[12:]
    s = pl.program_id(1)
    p = pl.program_id(2)
    d = HG_DIM

    @pl.when(p == 0)
    def _():
        xb_ref[...] = x_ref[...].astype(BF16)
        pa_ref[...] = _dot(xb_ref[...], jnp.concatenate([r[...] for r in w0_refs], axis=1))

    @pl.when(s == 0)
    def _():
        st_ref[2 * p] = jnp.zeros((d, d), F32)
        st_ref[2 * p + 1] = jnp.zeros((d, d), F32)

    e3 = e_ref[...]
    st_ref[2 * p] = _hg_head(layer, rows, pa_ref, lbl_ref[0], ng_ref[0], e3, m_ref,
                             st_ref[2 * p], o_ref, 0, qa_ref, ka_ref, xa_ref,
                             _proj_pieces(xb_ref, wodd_refs, pb_ref, _hg_n_fill(rows)))
    st_ref[2 * p + 1] = _hg_head(layer, rows, pb_ref, lbl_ref[1], ng_ref[1], e3, m_ref,
                                 st_ref[2 * p + 1], o_ref, d, qb_ref, kb_ref, xsb_ref,
                                 _proj_pieces(xb_ref, wnext_refs, pa_ref, _hg_n_fill(rows)))


def _hgrn2_mixer(x2d, batch, seq, layer, lb_logits, w_in, norm_g, rows):
    m = x2d.shape[0]
    nst = seq // rows
    d = HG_DIM
    pairs = HG_HEADS // 2
    w = w_in.astype(BF16)

    def head_cols(head_of):
        return [pl.BlockSpec((D_MODEL, d),
                             lambda b, s, p, sec=sec: (0, sec * HG_HEADS + head_of(p)))
                for sec in range(4)]

    lbl = lb_logits.astype(F32).reshape(DEPTH + 1, HG_HEADS, d).transpose(1, 0, 2)
    ng = norm_g.astype(F32).reshape(HG_HEADS, 1, d)
    e_np, m_np = _hg_tables()
    e_all = jnp.asarray(np.concatenate([e_np, e_np, e_np], axis=1), BF16)
    masks = jnp.asarray(m_np, F32)
    return pl.pallas_call(
        functools.partial(_hgrn2_kernel, layer, rows),
        grid=(batch, nst, pairs),
        in_specs=[
            pl.BlockSpec((rows, D_MODEL), lambda b, s, p: (b * nst + s, 0)),
            *head_cols(lambda p: 0),
            *head_cols(lambda p: 2 * p + 1),
            *head_cols(lambda p: jnp.minimum(2 * p + 2, HG_HEADS - 1)),
            pl.BlockSpec((2, DEPTH + 1, d), lambda b, s, p: (p, 0, 0)),
            pl.BlockSpec((2, 1, d), lambda b, s, p: (p, 0, 0)),
            pl.BlockSpec(e_all.shape, lambda b, s, p: (0, 0)),
            pl.BlockSpec(masks.shape, lambda b, s, p: (0, 0, 0)),
        ],
        out_specs=pl.BlockSpec((rows, 2 * d), lambda b, s, p: (b * nst + s, p)),
        out_shape=jax.ShapeDtypeStruct((m, HG_HEADS * d), BF16),
        scratch_shapes=[
            pltpu.VMEM((rows, D_MODEL), BF16),
            pltpu.VMEM((rows, 4 * d), F32),
            pltpu.VMEM((rows, 4 * d), F32),
            pltpu.VMEM((HG_HEADS, d, d), F32),
        ] + 2 * [
            pltpu.VMEM((rows, d), F32),
            pltpu.VMEM((rows, d), F32),
            pltpu.VMEM((rows // (2 * HG_CHUNK), 8 * HG_CHUNK, 2 * d), F32),
        ],
        compiler_params=_params("arbitrary", "arbitrary", "arbitrary"),
        name="hgrn2_mixer",
    )(x2d, *(12 * [w]), lbl, ng, e_all, masks)


_LN_UNROLL = 4


def _ln_rows(ln_rows, n_rows, z_of, g, b, o_ref, o16_ref=None):
    def body(r, carry):
        rs = pl.ds(pl.multiple_of(r * ln_rows, ln_rows), ln_rows)
        y = _layer_norm(z_of(rs), g, b)
        o_ref[rs, :] = y
        if o16_ref is not None:
            o16_ref[rs, :] = y.astype(BF16)
        return carry

    lax.fori_loop(0, n_rows // ln_rows, body, 0, unroll=_LN_UNROLL)


def _outproj_ln_kernel(rows, ln_rows, y0_ref, yodd_ref, ynext_ref, w_ref, x_ref, g_ref, b_ref,
                       o_ref, acca_ref, accb_ref):
    t = pl.program_id(0)

    @pl.when(t == 0)
    def _():
        acca_ref[...] = _dot(y0_ref[...], w_ref[...])

    g = g_ref[...]
    b = b_ref[...]

    def ln(acc_ref, row0):
        for r0 in range(0, rows, ln_rows):
            xs = slice(row0 + r0, row0 + r0 + ln_rows)
            z = ALPHA * x_ref[xs, :] + acc_ref[r0:r0 + ln_rows, :]
            o_ref[xs, :] = _layer_norm(z, g, b)

    accb_ref[...] = _dot(yodd_ref[...], w_ref[...])
    ln(acca_ref, 0)
    acca_ref[...] = _dot(ynext_ref[...], w_ref[...])
    ln(accb_ref, rows)


def _outproj_ln(y2d, w_out, x2d, g, b, rows, ln_rows):
    m = x2d.shape[0]
    n_sub = m // rows
    once = pl.Buffered(1)
    return pl.pallas_call(
        functools.partial(_outproj_ln_kernel, rows, ln_rows),
        grid=(n_sub // 2,),
        in_specs=[
            pl.BlockSpec((rows, D_MODEL), lambda t: (0, 0), pipeline_mode=once),
            pl.BlockSpec((rows, D_MODEL), lambda t: (2 * t + 1, 0)),
            pl.BlockSpec((rows, D_MODEL), lambda t: (jnp.minimum(2 * t + 2, n_sub - 1), 0)),
            pl.BlockSpec((D_MODEL, D_MODEL), lambda t: (0, 0), pipeline_mode=once),
            pl.BlockSpec((2 * rows, D_MODEL), lambda t: (t, 0)),
            pl.BlockSpec((1, D_MODEL), lambda t: (0, 0)),
            pl.BlockSpec((1, D_MODEL), lambda t: (0, 0)),
        ],
        out_specs=pl.BlockSpec((2 * rows, D_MODEL), lambda t: (t, 0)),
        out_shape=jax.ShapeDtypeStruct((m, D_MODEL), F32),
        scratch_shapes=[pltpu.VMEM((rows, D_MODEL), F32), pltpu.VMEM((rows, D_MODEL), F32)],
        compiler_params=_params("arbitrary"),
        name="outproj_ln",
    )(y2d, y2d, y2d, w_out.astype(BF16), x2d, g.astype(F32).reshape(1, D_MODEL),
      b.astype(F32).reshape(1, D_MODEL))


_HALO = BF16_SUBLANE_TILE
_CONV_ROWS = 64


def _ffn_kernel(rows, ln_rows, seq_tiles, with_bf16, x_ref, xp_ref, wa_ref, wb_ref, cw_ref,
                cb_ref, wd_ref, g_ref, b_ref, o_ref, *rest):
    o16_ref = rest[0] if with_bf16 else None
    xe_ref, ab_ref, hm_ref, acc_ref = rest[-4:]
    i = pl.program_id(0)
    j = pl.program_id(1)
    cols = hm_ref.shape[1]

    @pl.when(j == 0)
    def _():
        keep = (i % seq_tiles) != 0
        xe_ref[0:_HALO, :] = jnp.where(keep, xp_ref[...], 0.0).astype(BF16)
        xe_ref[_HALO:, :] = x_ref[...].astype(BF16)
        acc_ref[...] = jnp.zeros_like(acc_ref)

    ab_ref[...] = _dot(xe_ref[...], jnp.concatenate([wa_ref[...], wb_ref[...]], axis=1))
    cw = cw_ref[...]
    cb = cb_ref[...]
    for r0 in range(0, rows, _CONV_ROWS):
        a_0 = ab_ref[_HALO + r0:_HALO + r0 + _CONV_ROWS, 0:cols]
        a_1 = ab_ref[_HALO + r0 - 1:_HALO + r0 - 1 + _CONV_ROWS, 0:cols]
        a_2 = ab_ref[_HALO + r0 - 2:_HALO + r0 - 2 + _CONV_ROWS, 0:cols]
        conv = cw[0:1] * a_2 + cw[1:2] * a_1 + cw[2:3] * a_0 + cb
        hm = _silu(conv) * ab_ref[_HALO + r0:_HALO + r0 + _CONV_ROWS, cols:2 * cols]
        hm_ref[r0:r0 + _CONV_ROWS, :] = hm.astype(BF16)
    acc_ref[...] += _dot(hm_ref[...], wd_ref[...])

    @pl.when(j == pl.num_programs(1) - 1)
    def _():
        _ln_rows(ln_rows, rows, lambda rs: ALPHA * x_ref[rs, :] + acc_ref[rs, :],
                 g_ref[...], b_ref[...], o_ref, o16_ref)


def _conv_ffn_ln(x2d, seq, layer, w_up, conv_w, conv_b, w_down, g, b, rows, cols, ln_rows,
                 with_bf16):
    m = x2d.shape[0]
    nj = D_FF // cols
    halo_blocks = rows // _HALO
    out_spec = pl.BlockSpec((rows, D_MODEL), lambda i, j: (i, 0))
    out_f32 = jax.ShapeDtypeStruct((m, D_MODEL), F32)
    return pl.pallas_call(
        functools.partial(_ffn_kernel, rows, ln_rows, seq // rows, with_bf16),
        grid=(m // rows, nj),
        in_specs=[
            pl.BlockSpec((rows, D_MODEL), lambda i, j: (i, 0)),
            pl.BlockSpec((_HALO, D_MODEL), lambda i, j: (jnp.maximum(i * halo_blocks - 1, 0), 0)),
            pl.BlockSpec((None, D_MODEL, cols), lambda i, j: (layer, 0, j)),
            pl.BlockSpec((None, D_MODEL, cols), lambda i, j: (layer, 0, nj + j)),
            pl.BlockSpec((CONV_WIDTH, cols), lambda i, j: (0, j)),
            pl.BlockSpec((1, cols), lambda i, j: (0, j)),
            pl.BlockSpec((None, cols, D_MODEL), lambda i, j: (layer, j, 0)),
            pl.BlockSpec((1, D_MODEL), lambda i, j: (0, 0)),
            pl.BlockSpec((1, D_MODEL), lambda i, j: (0, 0)),
        ],
        out_specs=(out_spec, out_spec) if with_bf16 else out_spec,
        out_shape=((out_f32, jax.ShapeDtypeStruct((m, D_MODEL), BF16)) if with_bf16
                   else out_f32),
        scratch_shapes=[
            pltpu.VMEM((_HALO + rows, D_MODEL), BF16),
            pltpu.VMEM((_HALO + rows, 2 * cols), F32),
            pltpu.VMEM((rows, cols), BF16),
            pltpu.VMEM((rows, D_MODEL), F32),
        ],
        compiler_params=_params("arbitrary", "arbitrary"),
        name="conv_ffn_ln",
    )(x2d, x2d, w_up, w_up, conv_w.astype(F32), conv_b.astype(F32).reshape(1, D_FF),
      w_down, g.astype(F32).reshape(1, D_MODEL), b.astype(F32).reshape(1, D_MODEL))


def _gmlp_kernel(rows, ln_rows, xc_ref, xn_ref, wu_ref, wv_ref, lng_ref, lnb_ref, ws_ref,
                 bias_ref, o_ref, zv_ref, zu_ref, vn_ref, gate_ref, wc_ref):
    i = pl.program_id(0)
    c = SG_CHUNK
    d = SG_DIM

    @pl.when(i == 0)
    def _():
        causal = (lax.broadcasted_iota(jnp.int32, (c, c), 0)
                  >= lax.broadcasted_iota(jnp.int32, (c, c), 1))
        for g in range(SG_GROUPS):
            wc_ref[g] = jnp.where(causal, ws_ref[g], 0.0).astype(BF16)
        zv_ref[...] = _dot(xc_ref[...], wv_ref[...])

    zu_ref[...] = _dot(xc_ref[...], wu_ref[...])
    lng = lng_ref[...]
    lnb = lnb_ref[...]
    for r0 in range(0, rows, ln_rows):
        rs = slice(r0, r0 + ln_rows)
        vn_ref[rs, :] = _layer_norm(_gelu_exact(zv_ref[rs, :]), lng, lnb).astype(BF16)
    for r0 in range(0, rows, c):
        rs = slice(r0, r0 + c)
        for g in range(SG_GROUPS):
            gs = slice(g * d, (g + 1) * d)
            gate_ref[rs, gs] = _dot(wc_ref[g], vn_ref[rs, gs]) + bias_ref[:, gs]
    zv_ref[...] = _dot(xn_ref[...], wv_ref[...])
    for r0 in range(0, rows, ln_rows):
        rs = slice(r0, r0 + ln_rows)
        o_ref[rs, :] = (_gelu_exact(zu_ref[rs, :]) * gate_ref[rs, :]).astype(BF16)


def _gmlp_mixer(x16, w_in, ln_g, ln_b, w_s, b_s, rows, ln_rows):
    m = x16.shape[0]
    n_tiles = m // rows
    width = SG_GROUPS * SG_DIM
    bias = jnp.broadcast_to(b_s.astype(F32).T[:, :, None], (SG_CHUNK, SG_GROUPS, SG_DIM))
    bias = bias.reshape(SG_CHUNK, width)
    w_in = w_in.astype(BF16)
    once = pl.Buffered(1)
    return pl.pallas_call(
        functools.partial(_gmlp_kernel, rows, ln_rows),
        grid=(n_tiles,),
        in_specs=[
            pl.BlockSpec((rows, D_MODEL), lambda i: (i, 0)),
            pl.BlockSpec((rows, D_MODEL), lambda i: (jnp.minimum(i + 1, n_tiles - 1), 0)),
            pl.BlockSpec((D_MODEL, width), lambda i: (0, 0), pipeline_mode=once),
            pl.BlockSpec((D_MODEL, width), lambda i: (0, 1), pipeline_mode=once),
            pl.BlockSpec((1, width), lambda i: (0, 0)),
            pl.BlockSpec((1, width), lambda i: (0, 0)),
            pl.BlockSpec((SG_GROUPS, SG_CHUNK, SG_CHUNK), lambda i: (0, 0, 0), pipeline_mode=once),
            pl.BlockSpec((SG_CHUNK, width), lambda i: (0, 0), pipeline_mode=once),
        ],
        out_specs=pl.BlockSpec((rows, width), lambda i: (i, 0)),
        out_shape=jax.ShapeDtypeStruct((m, width), BF16),
        scratch_shapes=[
            pltpu.VMEM((rows, width), F32),
            pltpu.VMEM((rows, width), F32),
            pltpu.VMEM((rows, width), BF16),
            pltpu.VMEM((rows, width), F32),
            pltpu.VMEM((SG_GROUPS, SG_CHUNK, SG_CHUNK), BF16),
        ],
        compiler_params=_params("arbitrary"),
        name="gmlp_mixer",
    )(x16, x16, w_in, w_in, ln_g.astype(F32).reshape(1, width),
      ln_b.astype(F32).reshape(1, width), w_s.astype(F32), bias)


def kernel(x, lb_logits, hg_w_in, hg_norm_g, hg_w_out, sg_w_in, sg_ln_g, sg_ln_b, sg_w_s,
           sg_b_s, sg_w_out, ffn_w_up, ffn_conv_w, ffn_conv_b, ffn_w_down, ln1_g, ln1_b,
           ln2_g, ln2_b):
    batch, seq, _ = x.shape
    t = _tiles(batch, seq)
    h = x.astype(F32).reshape(batch * seq, D_MODEL)
    h16 = None
    w_up16 = ffn_w_up.astype(BF16)
    w_down16 = ffn_w_down.astype(BF16)
    for layer in range(DEPTH):
        occ = layer // N_MIXERS
        if layer % N_MIXERS == 0:
            mixed = _hgrn2_mixer(h, batch, seq, layer, lb_logits, hg_w_in[occ], hg_norm_g[occ],
                                 t["hg_rows"])
            w_out = hg_w_out[occ]
        else:
            mixed = _gmlp_mixer(h16 if h16 is not None else h.astype(BF16), sg_w_in[occ],
                                sg_ln_g[occ], sg_ln_b[occ], sg_w_s[occ], sg_b_s[occ],
                                t["sg_rows"], t["ln_rows"])
            w_out = sg_w_out[occ]
        h = _outproj_ln(mixed, w_out, h, ln1_g[layer], ln1_b[layer], t["out_rows"], t["ln_rows"])
        next_is_gmlp = layer + 1 < DEPTH and (layer + 1) % N_MIXERS == 1
        res = _conv_ffn_ln(h, seq, layer, w_up16, ffn_conv_w[layer], ffn_conv_b[layer],
                           w_down16, ln2_g[layer], ln2_b[layer],
                           t["ffn_rows"], t["ffn_cols"], t["ln_rows"], next_is_gmlp)
        h, h16 = res if next_is_gmlp else (res, None)
    return h.reshape(batch, seq, D_MODEL).astype(x.dtype)
```

```python
import functools

import numpy as np
import jax
import jax.numpy as jnp
from jax import lax
from jax.experimental import pallas as pl
from jax.experimental.pallas import tpu as pltpu

D_MODEL = 2048
DEPTH = 2
N_MIXERS = 2
HG_HEADS = 16
HG_DIM = 128
HG_CHUNK = 64
SG_GROUPS = 16
SG_DIM = 128
SG_CHUNK = 128
D_FF = 5632
CONV_WIDTH = 3
ALPHA = (2 * DEPTH) ** 0.25
LN_EPS = 1e-5
RMS_EPS = 1e-6

F32 = jnp.float32
BF16 = jnp.bfloat16

V7X_VMEM_LIMIT_BYTES = 56 * 1024 * 1024
BF16_SUBLANE_TILE = 16


def _tiles(batch, seq):
    rows = batch * seq
    t = 512
    assert seq % t == 0 and rows % t == 0 and D_FF % 512 == 0
    return dict(hg_rows=2 * t, out_rows=256, ffn_rows=t, ffn_cols=512, sg_rows=t, ln_rows=32)


def _sigmoid_pair(z):
    e = jnp.exp(-jnp.abs(z))
    r = 1.0 / (1.0 + e)
    er = e * r
    pos = z >= 0
    return jnp.where(pos, r, er), jnp.where(pos, er, r)


def _silu(z):
    return z / (1.0 + jnp.exp(-z))


def _gelu_exact(z):
    return 0.5 * z * (1.0 + lax.erf(z * (2.0 ** -0.5)))


def _layer_norm(z, g, b):
    mu = jnp.mean(z, axis=-1, keepdims=True)
    zc = z - mu
    var = jnp.mean(zc * zc, axis=-1, keepdims=True)
    return zc * lax.rsqrt(var + LN_EPS) * g + b


def _dot(a, b):
    return jnp.dot(a, b, preferred_element_type=F32)


def _dot_nt(a, b):
    return lax.dot_general(a, b, (((1,), (1,)), ((), ())), preferred_element_type=F32)


def _dot_tn(a, b):
    return lax.dot_general(a, b, (((0,), (0,)), ((), ())), preferred_element_type=F32)


def _params(*semantics):
    return pltpu.CompilerParams(dimension_semantics=semantics,
                                vmem_limit_bytes=V7X_VMEM_LIMIT_BYTES)


_HG_LEVELS = (32, 16, 8, 4, 2, 1)
_N_LEV = len(_HG_LEVELS)


def _hg_tables():
    c = HG_CHUNK
    e = np.zeros((8 * c, c), np.float32)
    masks = np.zeros((_N_LEV + 1, c, c), np.float32)
    t = np.arange(c)
    for li, h in enumerate(_HG_LEVELS):
        for tt in range(c):
            pos = tt % (2 * h)
            m = tt - pos + h
            if pos >= h:
                e[li * c + tt, m:tt + 1] = 1.0
            else:
                e[li * c + tt, tt + 1:m] = 1.0
        blk = t // (2 * h)
        upper = (t % (2 * h)) >= h
        masks[li] = (blk[:, None] == blk[None, :]) & upper[:, None] & (~upper)[None, :]
    masks[_N_LEV] = np.eye(c)
    e[6 * c:7 * c] = np.tril(np.ones((c, c)))
    e[7 * c:8 * c] = np.triu(np.ones((c, c)), 1)
    return e, masks


def _hg_head(layer, rows, proj_ref, lbl, ng, e3, m_ref, st, o_ref, lane0, q_ref, k_ref,
             xs_ref, fill):
    c = HG_CHUNK
    d = HG_DIM
    n_chunks = rows // c
    lrows = [lbl[i:i + 1, :] for i in range(DEPTH + 1)]
    mx = functools.reduce(jnp.maximum, lrows)
    ex = [jnp.exp(r - mx) for r in lrows]
    lb = sum(ex[:layer + 1]) / sum(ex)
    one_m_lb = 1.0 - lb

    def rs(ci):
        return slice(ci * c, (ci + 1) * c)

    def xs(ci, block):
        return xs_ref[ci // 2, block * c:(block + 1) * c, (ci % 2) * d:(ci % 2 + 1) * d]

    fill = iter(fill)

    def emit_fill():
        next(fill)()

    emit_fill()
    for pair in range(n_chunks // 2):
        lfs = []
        for ci in (2 * pair, 2 * pair + 1):
            q_ref[rs(ci), :] = _silu(proj_ref[rs(ci), 0:d])
            sig, sig_neg = _sigmoid_pair(proj_ref[rs(ci), d:2 * d])
            lfs.append(jnp.log(lb + one_m_lb * sig))
            k_ref[rs(ci), :] = one_m_lb * sig_neg
        lf2 = jnp.concatenate(lfs, axis=1)
        hi = lf2.astype(BF16)
        r1 = lf2 - hi.astype(F32)
        mid = r1.astype(BF16)
        lo = (r1 - mid.astype(F32)).astype(BF16)
        xs_ref[pair] = _dot(e3, jnp.concatenate([hi, mid, lo], axis=0))
    for _ in range(_hg_n_fill(rows) - 1):
        emit_fill()
    scores = []
    for ci in range(n_chunks):
        q = q_ref[rs(ci), :].astype(BF16)
        k = k_ref[rs(ci), :].astype(BF16)
        sc = _dot_nt(q, k) * m_ref[_N_LEV]
        for li in range(_N_LEV):
            w = jnp.exp(xs(ci, li)).astype(BF16)
            sc = sc + _dot_nt(q * w, k * w) * m_ref[li]
        scores.append(sc.astype(BF16))
    intra, kvs, qhs, decays = [], [], [], []
    for ci in range(n_chunks):
        v_b = proj_ref[rs(ci), 2 * d:3 * d].astype(BF16)
        cum = xs(ci, 6)
        kvs.append(_dot_tn(v_b, (k_ref[rs(ci), :] * jnp.exp(xs(ci, 7))).astype(BF16)))
        intra.append(_dot(scores[ci], v_b))
        qhs.append((q_ref[rs(ci), :] * jnp.exp(cum)).astype(BF16))
        decays.append(jnp.exp(cum[c - 1:c, :]))
    outs = []
    for ci in range(n_chunks):
        outs.append(intra[ci] + _dot_nt(qhs[ci], st.astype(BF16)))
        st = st * decays[ci] + kvs[ci]
    for ci in range(n_chunks):
        out = outs[ci]
        out = out * lax.rsqrt(jnp.mean(out * out, axis=-1, keepdims=True) + RMS_EPS)
        o_ref[rs(ci), lane0:lane0 + d] = (
            (out * ng * _silu(proj_ref[rs(ci), 3 * d:4 * d])).astype(BF16))
    return st


def _hg_n_fill(rows):
    return 4


def _proj_pieces(xb_ref, w_refs, proj_ref, n_pieces):
    kdim, d = w_refs[0].shape
    n_split = 2
    k_split = n_pieces // n_split
    kw = kdim // k_split
    pieces = []
    for ni in range(n_split):
        ns = slice(ni * 2 * d, (ni + 1) * 2 * d)
        for ki in range(k_split):
            ks = slice(ki * kw, (ki + 1) * kw)

            def piece(ni=ni, ns=ns, ks=ks, first=(ki == 0)):
                w = jnp.concatenate([w_refs[2 * ni][ks, :], w_refs[2 * ni + 1][ks, :]], axis=1)
                part = _dot(xb_ref[:, ks], w)
                if first:
                    proj_ref[:, ns] = part
                else:
                    proj_ref[:, ns] += part

            pieces.append(piece)
    return pieces


def _hgrn2_kernel(layer, rows, x_ref, *refs):
    w0_refs, wodd_refs, wnext_refs = refs[0:4], refs[4:8], refs[8:12]
    (lbl_ref, ng_ref, e_ref, m_ref, o_ref, xb_ref, pa_ref, pb_ref, st_ref, qa_ref, ka_ref,
     xa_ref, qb_ref, kb_ref, xsb_ref) = refs[12:]
    s = pl.program_id(1)
    p = pl.program_id(2)
    d = HG_DIM

    @pl.when(p == 0)
    def _():
        xb_ref[...] = x_ref[...].astype(BF16)
        pa_ref[...] = _dot(xb_ref[...], jnp.concatenate([r[...] for r in w0_refs], axis=1))

    @pl.when(s == 0)
    def _():
        st_ref[2 * p] = jnp.zeros((d, d), F32)
        st_ref[2 * p + 1] = jnp.zeros((d, d), F32)

    e3 = e_ref[...]
    st_ref[2 * p] = _hg_head(layer, rows, pa_ref, lbl_ref[0], ng_ref[0], e3, m_ref,
                             st_ref[2 * p], o_ref, 0, qa_ref, ka_ref, xa_ref,
                             _proj_pieces(xb_ref, wodd_refs, pb_ref, _hg_n_fill(rows)))
    st_ref[2 * p + 1] = _hg_head(layer, rows, pb_ref, lbl_ref[1], ng_ref[1], e3, m_ref,
                                 st_ref[2 * p + 1], o_ref, d, qb_ref, kb_ref, xsb_ref,
                                 _proj_pieces(xb_ref, wnext_refs, pa_ref, _hg_n_fill(rows)))


def _hgrn2_mixer(x2d, batch, seq, layer, lb_logits, w_in, norm_g, rows):
    m = x2d.shape[0]
    nst = seq // rows
    d = HG_DIM
    pairs = HG_HEADS // 2
    w = w_in.astype(BF16)

    def head_cols(head_of):
        return [pl.BlockSpec((D_MODEL, d),
                             lambda b, s, p, sec=sec: (0, sec * HG_HEADS + head_of(p)))
                for sec in range(4)]

    lbl = lb_logits.astype(F32).reshape(DEPTH + 1, HG_HEADS, d).transpose(1, 0, 2)
    ng = norm_g.astype(F32).reshape(HG_HEADS, 1, d)
    e_np, m_np = _hg_tables()
    e_all = jnp.asarray(np.concatenate([e_np, e_np, e_np], axis=1), BF16)
    masks = jnp.asarray(m_np, F32)
    return pl.pallas_call(
        functools.partial(_hgrn2_kernel, layer, rows),
        grid=(batch, nst, pairs),
        in_specs=[
            pl.BlockSpec((rows, D_MODEL), lambda b, s, p: (b * nst + s, 0)),
            *head_cols(lambda p: 0),
            *head_cols(lambda p: 2 * p + 1),
            *head_cols(lambda p: jnp.minimum(2 * p + 2, HG_HEADS - 1)),
            pl.BlockSpec((2, DEPTH + 1, d), lambda b, s, p: (p, 0, 0)),
            pl.BlockSpec((2, 1, d), lambda b, s, p: (p, 0, 0)),
            pl.BlockSpec(e_all.shape, lambda b, s, p: (0, 0)),
            pl.BlockSpec(masks.shape, lambda b, s, p: (0, 0, 0)),
        ],
        out_specs=pl.BlockSpec((rows, 2 * d), lambda b, s, p: (b * nst + s, p)),
        out_shape=jax.ShapeDtypeStruct((m, HG_HEADS * d), BF16),
        scratch_shapes=[
            pltpu.VMEM((rows, D_MODEL), BF16),
            pltpu.VMEM((rows, 4 * d), F32),
            pltpu.VMEM((rows, 4 * d), F32),
            pltpu.VMEM((HG_HEADS, d, d), F32),
        ] + 2 * [
            pltpu.VMEM((rows, d), F32),
            pltpu.VMEM((rows, d), F32),
            pltpu.VMEM((rows // (2 * HG_CHUNK), 8 * HG_CHUNK, 2 * d), F32),
        ],
        compiler_params=_params("arbitrary", "arbitrary", "arbitrary"),
        name="hgrn2_mixer",
    )(x2d, *(12 * [w]), lbl, ng, e_all, masks)


_LN_UNROLL = 4


def _ln_rows(ln_rows, n_rows, z_of, g, b, o_ref, o16_ref=None):
    def body(r, carry):
        rs = pl.ds(pl.multiple_of(r * ln_rows, ln_rows), ln_rows)
        y = _layer_norm(z_of(rs), g, b)
        o_ref[rs, :] = y
        if o16_ref is not None:
            o16_ref[rs, :] = y.astype(BF16)
        return carry

    lax.fori_loop(0, n_rows // ln_rows, body, 0, unroll=_LN_UNROLL)


def _outproj_ln_kernel(rows, ln_rows, y0_ref, yodd_ref, ynext_ref, w_ref, x_ref, g_ref, b_ref,
                       o_ref, acca_ref, accb_ref):
    t = pl.program_id(0)

    @pl.when(t == 0)
    def _():
        acca_ref[...] = _dot(y0_ref[...], w_ref[...])

    g = g_ref[...]
    b = b_ref[...]

    def ln(acc_ref, row0):
        for r0 in range(0, rows, ln_rows):
            xs = slice(row0 + r0, row0 + r0 + ln_rows)
            z = ALPHA * x_ref[xs, :] + acc_ref[r0:r0 + ln_rows, :]
            o_ref[xs, :] = _layer_norm(z, g, b)

    accb_ref[...] = _dot(yodd_ref[...], w_ref[...])
    ln(acca_ref, 0)
    acca_ref[...] = _dot(ynext_ref[...], w_ref[...])
    ln(accb_ref, rows)


def _outproj_ln(y2d, w_out, x2d, g, b, rows, ln_rows):
    m = x2d.shape[0]
    n_sub = m // rows
    once = pl.Buffered(1)
    return pl.pallas_call(
        functools.partial(_outproj_ln_kernel, rows, ln_rows),
        grid=(n_sub // 2,),
        in_specs=[
            pl.BlockSpec((rows, D_MODEL), lambda t: (0, 0), pipeline_mode=once),
            pl.BlockSpec((rows, D_MODEL), lambda t: (2 * t + 1, 0)),
            pl.BlockSpec((rows, D_MODEL), lambda t: (jnp.minimum(2 * t + 2, n_sub - 1), 0)),
            pl.BlockSpec((D_MODEL, D_MODEL), lambda t: (0, 0), pipeline_mode=once),
            pl.BlockSpec((2 * rows, D_MODEL), lambda t: (t, 0)),
            pl.BlockSpec((1, D_MODEL), lambda t: (0, 0)),
            pl.BlockSpec((1, D_MODEL), lambda t: (0, 0)),
        ],
        out_specs=pl.BlockSpec((2 * rows, D_MODEL), lambda t: (t, 0)),
        out_shape=jax.ShapeDtypeStruct((m, D_MODEL), F32),
        scratch_shapes=[pltpu.VMEM((rows, D_MODEL), F32), pltpu.VMEM((rows, D_MODEL), F32)],
        compiler_params=_params("arbitrary"),
        name="outproj_ln",
    )(y2d, y2d, y2d, w_out.astype(BF16), x2d, g.astype(F32).reshape(1, D_MODEL),
      b.astype(F32).reshape(1, D_MODEL))


_HALO = BF16_SUBLANE_TILE
_CONV_ROWS = 64


def _ffn_kernel(rows, ln_rows, seq_tiles, cols, layer, with_bf16, x_ref, xp_ref, wup_hbm,
                wdn_hbm, cw_ref, cb_ref, g_ref, b_ref, o_ref, *rest):
    o16_ref = rest[0] if with_bf16 else None
    xe_ref, a_ref, bb_ref, hm_ref, acc_ref, wa_buf, wb_buf, wd_buf, sem = rest[-9:]
    i = pl.program_id(0)
    nj = D_FF // cols

    def weight_copies(j, slot):
        c0 = pl.multiple_of(j * cols, cols)
        return (
            pltpu.make_async_copy(wup_hbm.at[layer, :, pl.ds(c0, cols)], wa_buf.at[slot],
                                  sem.at[0, slot]),
            pltpu.make_async_copy(wup_hbm.at[layer, :, pl.ds(D_FF + c0, cols)], wb_buf.at[slot],
                                  sem.at[1, slot]),
            pltpu.make_async_copy(wdn_hbm.at[layer, pl.ds(c0, cols), :], wd_buf.at[slot],
                                  sem.at[2, slot]),
        )

    for cp in weight_copies(0, 0):
        cp.start()
    keep = (i % seq_tiles) != 0
    xe_ref[0:_HALO, :] = jnp.where(keep, xp_ref[...], 0.0).astype(BF16)
    xe_ref[_HALO:, :] = x_ref[...].astype(BF16)
    acc_ref[...] = jnp.zeros_like(acc_ref)

    def col_tile(j, carry):
        slot = j % 2

        @pl.when(j + 1 < nj)
        def _():
            for cp in weight_copies(j + 1, 1 - slot):
                cp.start()

        for cp in weight_copies(j, slot):
            cp.wait()
        a_ref[...] = _dot(xe_ref[...], wa_buf[slot])
        bb_ref[...] = _dot(xe_ref[_HALO:, :], wb_buf[slot])
        cw = cw_ref[j]
        cb = cb_ref[j]
        for r0 in range(0, rows, _CONV_ROWS):
            a_0 = a_ref[_HALO + r0:_HALO + r0 + _CONV_ROWS, :]
            a_1 = a_ref[_HALO + r0 - 1:_HALO + r0 - 1 + _CONV_ROWS, :]
            a_2 = a_ref[_HALO + r0 - 2:_HALO + r0 - 2 + _CONV_ROWS, :]
            conv = cw[0:1] * a_2 + cw[1:2] * a_1 + cw[2:3] * a_0 + cb
            hm = _silu(conv) * bb_ref[r0:r0 + _CONV_ROWS, :]
            hm_ref[r0:r0 + _CONV_ROWS, :] = hm.astype(BF16)
        acc_ref[...] += _dot(hm_ref[...], wd_buf[slot])
        return carry

    lax.fori_loop(0, nj, col_tile, 0)
    _ln_rows(ln_rows, rows, lambda rs: ALPHA * x_ref[rs, :] + acc_ref[rs, :],
             g_ref[...], b_ref[...], o_ref, o16_ref)


def _conv_ffn_ln(x2d, seq, layer, w_up, conv_w, conv_b, w_down, g, b, rows, cols, ln_rows,
                 with_bf16):
    m = x2d.shape[0]
    nj = D_FF // cols
    halo_blocks = rows // _HALO
    out_spec = pl.BlockSpec((rows, D_MODEL), lambda i: (i, 0))
    out_f32 = jax.ShapeDtypeStruct((m, D_MODEL), F32)
    cw = conv_w.astype(F32).reshape(CONV_WIDTH, nj, cols).transpose(1, 0, 2)
    cb = conv_b.astype(F32).reshape(nj, 1, cols)
    once = pl.Buffered(1)
    return pl.pallas_call(
        functools.partial(_ffn_kernel, rows, ln_rows, seq // rows, cols, layer, with_bf16),
        grid=(m // rows,),
        in_specs=[
            pl.BlockSpec((rows, D_MODEL), lambda i: (i, 0)),
            pl.BlockSpec((_HALO, D_MODEL), lambda i: (jnp.maximum(i * halo_blocks - 1, 0), 0)),
            pl.BlockSpec(memory_space=pl.ANY),
            pl.BlockSpec(memory_space=pl.ANY),
            pl.BlockSpec((nj, CONV_WIDTH, cols), lambda i: (0, 0, 0), pipeline_mode=once),
            pl.BlockSpec((nj, 1, cols), lambda i: (0, 0, 0), pipeline_mode=once),
            pl.BlockSpec((1, D_MODEL), lambda i: (0, 0)),
            pl.BlockSpec((1, D_MODEL), lambda i: (0, 0)),
        ],
        out_specs=(out_spec, out_spec) if with_bf16 else out_spec,
        out_shape=((out_f32, jax.ShapeDtypeStruct((m, D_MODEL), BF16)) if with_bf16
                   else out_f32),
        scratch_shapes=[
            pltpu.VMEM((_HALO + rows, D_MODEL), BF16),
            pltpu.VMEM((_HALO + rows, cols), F32),
            pltpu.VMEM((rows, cols), F32),
            pltpu.VMEM((rows, cols), BF16),
            pltpu.VMEM((rows, D_MODEL), F32),
            pltpu.VMEM((2, D_MODEL, cols), BF16),
            pltpu.VMEM((2, D_MODEL, cols), BF16),
            pltpu.VMEM((2, cols, D_MODEL), BF16),
            pltpu.SemaphoreType.DMA((3, 2)),
        ],
        compiler_params=_params("arbitrary"),
        name="conv_ffn_ln",
    )(x2d, x2d, w_up, w_down, cw, cb, g.astype(F32).reshape(1, D_MODEL),
      b.astype(F32).reshape(1, D_MODEL))


def _gmlp_kernel(rows, ln_rows, xc_ref, xn_ref, wu_ref, wv_ref, lng_ref, lnb_ref, ws_ref,
                 bias_ref, o_ref, zv_ref, zu_ref, vn_ref, gate_ref, wc_ref):
    i = pl.program_id(0)
    c = SG_CHUNK
    d = SG_DIM

    @pl.when(i == 0)
    def _():
        causal = (lax.broadcasted_iota(jnp.int32, (c, c), 0)
                  >= lax.broadcasted_iota(jnp.int32, (c, c), 1))
        for g in range(SG_GROUPS):
            wc_ref[g] = jnp.where(causal, ws_ref[g], 0.0).astype(BF16)
        zv_ref[...] = _dot(xc_ref[...], wv_ref[...])

    zu_ref[...] = _dot(xc_ref[...], wu_ref[...])
    lng = lng_ref[...]
    lnb = lnb_ref[...]
    for r0 in range(0, rows, ln_rows):
        rs = slice(r0, r0 + ln_rows)
        vn_ref[rs, :] = _layer_norm(_gelu_exact(zv_ref[rs, :]), lng, lnb).astype(BF16)
    for r0 in range(0, rows, c):
        rs = slice(r0, r0 + c)
        for g in range(SG_GROUPS):
            gs = slice(g * d, (g + 1) * d)
            gate_ref[rs, gs] = _dot(wc_ref[g], vn_ref[rs, gs]) + bias_ref[:, gs]
    zv_ref[...] = _dot(xn_ref[...], wv_ref[...])
    for r0 in range(0, rows, ln_rows):
        rs = slice(r0, r0 + ln_rows)
        o_ref[rs, :] = (_gelu_exact(zu_ref[rs, :]) * gate_ref[rs, :]).astype(BF16)


def _gmlp_mixer(x16, w_in, ln_g, ln_b, w_s, b_s, rows, ln_rows):
    m = x16.shape[0]
    n_tiles = m // rows
    width = SG_GROUPS * SG_DIM
    bias = jnp.broadcast_to(b_s.astype(F32).T[:, :, None], (SG_CHUNK, SG_GROUPS, SG_DIM))
    bias = bias.reshape(SG_CHUNK, width)
    w_in = w_in.astype(BF16)
    once = pl.Buffered(1)
    return pl.pallas_call(
        functools.partial(_gmlp_kernel, rows, ln_rows),
        grid=(n_tiles,),
        in_specs=[
            pl.BlockSpec((rows, D_MODEL), lambda i: (i, 0)),
            pl.BlockSpec((rows, D_MODEL), lambda i: (jnp.minimum(i + 1, n_tiles - 1), 0)),
            pl.BlockSpec((D_MODEL, width), lambda i: (0, 0), pipeline_mode=once),
            pl.BlockSpec((D_MODEL, width), lambda i: (0, 1), pipeline_mode=once),
            pl.BlockSpec((1, width), lambda i: (0, 0)),
            pl.BlockSpec((1, width), lambda i: (0, 0)),
            pl.BlockSpec((SG_GROUPS, SG_CHUNK, SG_CHUNK), lambda i: (0, 0, 0), pipeline_mode=once),
            pl.BlockSpec((SG_CHUNK, width), lambda i: (0, 0), pipeline_mode=once),
        ],
        out_specs=pl.BlockSpec((rows, width), lambda i: (i, 0)),
        out_shape=jax.ShapeDtypeStruct((m, width), BF16),
        scratch_shapes=[
            pltpu.VMEM((rows, width), F32),
            pltpu.VMEM((rows, width), F32),
            pltpu.VMEM((rows, width), BF16),
            pltpu.VMEM((rows, width), F32),
            pltpu.VMEM((SG_GROUPS, SG_CHUNK, SG_CHUNK), BF16),
        ],
        compiler_params=_params("arbitrary"),
        name="gmlp_mixer",
    )(x16, x16, w_in, w_in, ln_g.astype(F32).reshape(1, width),
      ln_b.astype(F32).reshape(1, width), w_s.astype(F32), bias)


def kernel(x, lb_logits, hg_w_in, hg_norm_g, hg_w_out, sg_w_in, sg_ln_g, sg_ln_b, sg_w_s,
           sg_b_s, sg_w_out, ffn_w_up, ffn_conv_w, ffn_conv_b, ffn_w_down, ln1_g, ln1_b,
           ln2_g, ln2_b):
    batch, seq, _ = x.shape
    t = _tiles(batch, seq)
    h = x.astype(F32).reshape(batch * seq, D_MODEL)
    h16 = None
    w_up16 = ffn_w_up.astype(BF16)
    w_down16 = ffn_w_down.astype(BF16)
    for layer in range(DEPTH):
        occ = layer // N_MIXERS
        if layer % N_MIXERS == 0:
            mixed = _hgrn2_mixer(h, batch, seq, layer, lb_logits, hg_w_in[occ], hg_norm_g[occ],
                                 t["hg_rows"])
            w_out = hg_w_out[occ]
        else:
            mixed = _gmlp_mixer(h16 if h16 is not None else h.astype(BF16), sg_w_in[occ],
                                sg_ln_g[occ], sg_ln_b[occ], sg_w_s[occ], sg_b_s[occ],
                                t["sg_rows"], t["ln_rows"])
            w_out = sg_w_out[occ]
        h = _outproj_ln(mixed, w_out, h, ln1_g[layer], ln1_b[layer], t["out_rows"], t["ln_rows"])
        next_is_gmlp = layer + 1 < DEPTH and (layer + 1) % N_MIXERS == 1
        res = _conv_ffn_ln(h, seq, layer, w_up16, ffn_conv_w[layer], ffn_conv_b[layer],
                           w_down16, ln2_g[layer], ln2_b[layer],
                           t["ffn_rows"], t["ffn_cols"], t["ln_rows"], next_is_gmlp)
        h, h16 = res if next_is_gmlp else (res, None)
    return h.reshape(batch, seq, D_MODEL).astype(x.dtype)
```

```python
import functools

import numpy as np
import jax
import jax.numpy as jnp
from jax import lax
from jax.experimental import pallas as pl
from jax.experimental.pallas import tpu as pltpu

D_MODEL = 2048
DEPTH = 2
N_MIXERS = 2
HG_HEADS = 16
HG_DIM = 128
HG_CHUNK = 64
SG_GROUPS = 16
SG_DIM = 128
SG_CHUNK = 128
D_FF = 5632
CONV_WIDTH = 3
ALPHA = (2 * DEPTH) ** 0.25
LN_EPS = 1e-5
RMS_EPS = 1e-6

F32 = jnp.float32
BF16 = jnp.bfloat16

V7X_VMEM_LIMIT_BYTES = 56 * 1024 * 1024
BF16_SUBLANE_TILE = 16


def _tiles(batch, seq):
    rows = batch * seq
    t = 512
    assert seq % t == 0 and rows % t == 0 and D_FF % 512 == 0
    return dict(hg_rows=2 * t, out_rows=256, ffn_rows=t, ffn_cols=512, sg_rows=t, ln_rows=32)


def _sigmoid_pair(z):
    e = jnp.exp(-jnp.abs(z))
    r = 1.0 / (1.0 + e)
    er = e * r
    pos = z >= 0
    return jnp.where(pos, r, er), jnp.where(pos, er, r)


def _silu(z):
    return z / (1.0 + jnp.exp(-z))


def _gelu_exact(z):
    return 0.5 * z * (1.0 + lax.erf(z * (2.0 ** -0.5)))


def _layer_norm(z, g, b):
    mu = jnp.mean(z, axis=-1, keepdims=True)
    zc = z - mu
    var = jnp.mean(zc * zc, axis=-1, keepdims=True)
    return zc * lax.rsqrt(var + LN_EPS) * g + b


def _dot(a, b):
    return jnp.dot(a, b, preferred_element_type=F32)


def _dot_nt(a, b):
    return lax.dot_general(a, b, (((1,), (1,)), ((), ())), preferred_element_type=F32)


def _dot_tn(a, b):
    return lax.dot_general(a, b, (((0,), (0,)), ((), ())), preferred_element_type=F32)


def _params(*semantics):
    return pltpu.CompilerParams(dimension_semantics=semantics,
                                vmem_limit_bytes=V7X_VMEM_LIMIT_BYTES)


_HG_LEVELS = (32, 16, 8, 4, 2, 1)
_N_LEV = len(_HG_LEVELS)


def _hg_tables():
    c = HG_CHUNK
    e = np.zeros(((_N_LEV + 1) * c, c), np.float32)
    masks = np.zeros((_N_LEV + 1, c, c), np.float32)
    t = np.arange(c)
    for li, h in enumerate(_HG_LEVELS):
        for tt in range(c):
            pos = tt % (2 * h)
            m = tt - pos + h
            if pos >= h:
                e[li * c + tt, m:tt + 1] = 1.0
            else:
                e[li * c + tt, tt + 1:m] = 1.0
        blk = t // (2 * h)
        upper = (t % (2 * h)) >= h
        masks[li] = (blk[:, None] == blk[None, :]) & upper[:, None] & (~upper)[None, :]
    masks[_N_LEV] = np.eye(c)
    e[_N_LEV * c:(_N_LEV + 1) * c] = np.tril(np.ones((c, c)))
    return e, masks


def _hg_head(layer, rows, proj_ref, lbl, ng, e3, m_ref, st, o_ref, lane0, q_ref, k_ref,
             xs_ref, fill):
    c = HG_CHUNK
    d = HG_DIM
    n_chunks = rows // c
    lrows = [lbl[i:i + 1, :] for i in range(DEPTH + 1)]
    mx = functools.reduce(jnp.maximum, lrows)
    ex = [jnp.exp(r - mx) for r in lrows]
    lb = sum(ex[:layer + 1]) / sum(ex)
    one_m_lb = 1.0 - lb

    def rs(ci):
        return slice(ci * c, (ci + 1) * c)

    def xs(ci, block):
        return xs_ref[ci // 2, block * c:(block + 1) * c, (ci % 2) * d:(ci % 2 + 1) * d]

    fill = iter(fill)

    def emit_fill():
        next(fill)()

    emit_fill()
    for pair in range(n_chunks // 2):
        lfs = []
        for ci in (2 * pair, 2 * pair + 1):
            q_ref[rs(ci), :] = _silu(proj_ref[rs(ci), 0:d])
            sig, sig_neg = _sigmoid_pair(proj_ref[rs(ci), d:2 * d])
            lfs.append(jnp.log(lb + one_m_lb * sig))
            k_ref[rs(ci), :] = one_m_lb * sig_neg
        lf2 = jnp.concatenate(lfs, axis=1)
        hi = lf2.astype(BF16)
        r1 = lf2 - hi.astype(F32)
        mid = r1.astype(BF16)
        lo = (r1 - mid.astype(F32)).astype(BF16)
        xs_ref[pair] = _dot(e3, jnp.concatenate([hi, mid, lo], axis=0))
    for _ in range(_hg_n_fill(rows) - 1):
        emit_fill()
    scores = []
    for ci in range(n_chunks):
        q = q_ref[rs(ci), :].astype(BF16)
        k = k_ref[rs(ci), :].astype(BF16)
        w = jnp.exp(xs(ci, _N_LEV - 1)).astype(BF16)
        both = _dot_nt(jnp.concatenate([q, q * w], axis=0), k)
        sc = both[0:c] * m_ref[_N_LEV] + both[c:2 * c] * m_ref[_N_LEV - 1]
        for li in range(_N_LEV - 1):
            w = jnp.exp(xs(ci, li)).astype(BF16)
            sc = sc + _dot_nt(q * w, k * w) * m_ref[li]
        scores.append(sc.astype(BF16))
    intra, kvs, qhs, decays = [], [], [], []
    for ci in range(n_chunks):
        v_b = proj_ref[rs(ci), 2 * d:3 * d].astype(BF16)
        cum = xs(ci, _N_LEV)
        k_hat = k_ref[rs(ci), :] * jnp.exp(cum[c - 1:c, :] - cum)
        kvs.append(_dot_tn(v_b, k_hat.astype(BF16)))
        intra.append(_dot(scores[ci], v_b))
        qhs.append((q_ref[rs(ci), :] * jnp.exp(cum)).astype(BF16))
        decays.append(jnp.exp(cum[c - 1:c, :]))
    outs = []
    for ci in range(n_chunks):
        outs.append(intra[ci] + _dot_nt(qhs[ci], st.astype(BF16)))
        st = st * decays[ci] + kvs[ci]
    for ci in range(n_chunks):
        out = outs[ci]
        out = out * lax.rsqrt(jnp.mean(out * out, axis=-1, keepdims=True) + RMS_EPS)
        o_ref[rs(ci), lane0:lane0 + d] = (
            (out * ng * _silu(proj_ref[rs(ci), 3 * d:4 * d])).astype(BF16))
    return st


def _hg_n_fill(rows):
    return 4


def _proj_pieces(xb_ref, w_refs, proj_ref, n_pieces):
    kdim, d = w_refs[0].shape
    n_split = 2
    k_split = n_pieces // n_split
    kw = kdim // k_split
    pieces = []
    for ni in range(n_split):
        ns = slice(ni * 2 * d, (ni + 1) * 2 * d)
        for ki in range(k_split):
            ks = slice(ki * kw, (ki + 1) * kw)

            def piece(ni=ni, ns=ns, ks=ks, first=(ki == 0)):
                w = jnp.concatenate([w_refs[2 * ni][ks, :], w_refs[2 * ni + 1][ks, :]], axis=1)
                part = _dot(xb_ref[:, ks], w)
                if first:
                    proj_ref[:, ns] = part
                else:
                    proj_ref[:, ns] += part

            pieces.append(piece)
    return pieces


def _hgrn2_kernel(layer, rows, x_ref, *refs):
    w0_refs, wodd_refs, wnext_refs = refs[0:4], refs[4:8], refs[8:12]
    (lbl_ref, ng_ref, e_ref, m_ref, o_ref, xb_ref, pa_ref, pb_ref, st_ref, qa_ref, ka_ref,
     xa_ref, qb_ref, kb_ref, xsb_ref) = refs[12:]
    s = pl.program_id(1)
    p = pl.program_id(2)
    d = HG_DIM

    @pl.when(p == 0)
    def _():
        xb_ref[...] = x_ref[...].astype(BF16)
        pa_ref[...] = _dot(xb_ref[...], jnp.concatenate([r[...] for r in w0_refs], axis=1))

    @pl.when(s == 0)
    def _():
        st_ref[2 * p] = jnp.zeros((d, d), F32)
        st_ref[2 * p + 1] = jnp.zeros((d, d), F32)

    e3 = e_ref[...]
    st_ref[2 * p] = _hg_head(layer, rows, pa_ref, lbl_ref[0], ng_ref[0], e3, m_ref,
                             st_ref[2 * p], o_ref, 0, qa_ref, ka_ref, xa_ref,
                             _proj_pieces(xb_ref, wodd_refs, pb_ref, _hg_n_fill(rows)))
    st_ref[2 * p + 1] = _hg_head(layer, rows, pb_ref, lbl_ref[1], ng_ref[1], e3, m_ref,
                                 st_ref[2 * p + 1], o_ref, d, qb_ref, kb_ref, xsb_ref,
                                 _proj_pieces(xb_ref, wnext_refs, pa_ref, _hg_n_fill(rows)))


def _hgrn2_mixer(x2d, batch, seq, layer, lb_logits, w_in, norm_g, rows):
    m = x2d.shape[0]
    nst = seq // rows
    d = HG_DIM
    pairs = HG_HEADS // 2
    w = w_in.astype(BF16)

    def head_cols(head_of):
        return [pl.BlockSpec((D_MODEL, d),
                             lambda b, s, p, sec=sec: (0, sec * HG_HEADS + head_of(p)))
                for sec in range(4)]

    lbl = lb_logits.astype(F32).reshape(DEPTH + 1, HG_HEADS, d).transpose(1, 0, 2)
    ng = norm_g.astype(F32).reshape(HG_HEADS, 1, d)
    e_np, m_np = _hg_tables()
    e_all = jnp.asarray(np.concatenate([e_np, e_np, e_np], axis=1), BF16)
    masks = jnp.asarray(m_np, F32)
    return pl.pallas_call(
        functools.partial(_hgrn2_kernel, layer, rows),
        grid=(batch, nst, pairs),
        in_specs=[
            pl.BlockSpec((rows, D_MODEL), lambda b, s, p: (b * nst + s, 0)),
            *head_cols(lambda p: 0),
            *head_cols(lambda p: 2 * p + 1),
            *head_cols(lambda p: jnp.minimum(2 * p + 2, HG_HEADS - 1)),
            pl.BlockSpec((2, DEPTH + 1, d), lambda b, s, p: (p, 0, 0)),
            pl.BlockSpec((2, 1, d), lambda b, s, p: (p, 0, 0)),
            pl.BlockSpec(e_all.shape, lambda b, s, p: (0, 0)),
            pl.BlockSpec(masks.shape, lambda b, s, p: (0, 0, 0)),
        ],
        out_specs=pl.BlockSpec((rows, 2 * d), lambda b, s, p: (b * nst + s, p)),
        out_shape=jax.ShapeDtypeStruct((m, HG_HEADS * d), BF16),
        scratch_shapes=[
            pltpu.VMEM((rows, D_MODEL), BF16),
            pltpu.VMEM((rows, 4 * d), F32),
            pltpu.VMEM((rows, 4 * d), F32),
            pltpu.VMEM((HG_HEADS, d, d), F32),
        ] + 2 * [
            pltpu.VMEM((rows, d), F32),
            pltpu.VMEM((rows, d), F32),
            pltpu.VMEM((rows // (2 * HG_CHUNK), (_N_LEV + 1) * HG_CHUNK, 2 * d), F32),
        ],
        compiler_params=_params("arbitrary", "arbitrary", "arbitrary"),
        name="hgrn2_mixer",
    )(x2d, *(12 * [w]), lbl, ng, e_all, masks)


_LN_UNROLL = 4


def _ln_rows(ln_rows, n_rows, z_of, g, b, o_ref, o16_ref=None):
    def body(r, carry):
        rs = pl.ds(pl.multiple_of(r * ln_rows, ln_rows), ln_rows)
        y = _layer_norm(z_of(rs), g, b)
        o_ref[rs, :] = y
        if o16_ref is not None:
            o16_ref[rs, :] = y.astype(BF16)
        return carry

    lax.fori_loop(0, n_rows // ln_rows, body, 0, unroll=_LN_UNROLL)


def _outproj_ln_kernel(rows, ln_rows, y0_ref, yodd_ref, ynext_ref, w_ref, x_ref, g_ref, b_ref,
                       o_ref, acca_ref, accb_ref):
    t = pl.program_id(0)

    @pl.when(t == 0)
    def _():
        acca_ref[...] = _dot(y0_ref[...], w_ref[...])

    g = g_ref[...]
    b = b_ref[...]

    def ln(acc_ref, row0):
        for r0 in range(0, rows, ln_rows):
            xs = slice(row0 + r0, row0 + r0 + ln_rows)
            z = ALPHA * x_ref[xs, :] + acc_ref[r0:r0 + ln_rows, :]
            o_ref[xs, :] = _layer_norm(z, g, b)

    accb_ref[...] = _dot(yodd_ref[...], w_ref[...])
    ln(acca_ref, 0)
    acca_ref[...] = _dot(ynext_ref[...], w_ref[...])
    ln(accb_ref, rows)


def _outproj_ln(y2d, w_out, x2d, g, b, rows, ln_rows):
    m = x2d.shape[0]
    n_sub = m // rows
    once = pl.Buffered(1)
    return pl.pallas_call(
        functools.partial(_outproj_ln_kernel, rows, ln_rows),
        grid=(n_sub // 2,),
        in_specs=[
            pl.BlockSpec((rows, D_MODEL), lambda t: (0, 0), pipeline_mode=once),
            pl.BlockSpec((rows, D_MODEL), lambda t: (2 * t + 1, 0)),
            pl.BlockSpec((rows, D_MODEL), lambda t: (jnp.minimum(2 * t + 2, n_sub - 1), 0)),
            pl.BlockSpec((D_MODEL, D_MODEL), lambda t: (0, 0), pipeline_mode=once),
            pl.BlockSpec((2 * rows, D_MODEL), lambda t: (t, 0)),
            pl.BlockSpec((1, D_MODEL), lambda t: (0, 0)),
            pl.BlockSpec((1, D_MODEL), lambda t: (0, 0)),
        ],
        out_specs=pl.BlockSpec((2 * rows, D_MODEL), lambda t: (t, 0)),
        out_shape=jax.ShapeDtypeStruct((m, D_MODEL), F32),
        scratch_shapes=[pltpu.VMEM((rows, D_MODEL), F32), pltpu.VMEM((rows, D_MODEL), F32)],
        compiler_params=_params("arbitrary"),
        name="outproj_ln",
    )(y2d, y2d, y2d, w_out.astype(BF16), x2d, g.astype(F32).reshape(1, D_MODEL),
      b.astype(F32).reshape(1, D_MODEL))


_HALO = BF16_SUBLANE_TILE
_CONV_ROWS = 64


def _ffn_kernel(rows, ln_rows, seq_tiles, with_bf16, x_ref, xp_ref, wa_ref, wb_ref, cw_ref,
                cb_ref, wd_ref, g_ref, b_ref, o_ref, *rest):
    o16_ref = rest[0] if with_bf16 else None
    xe_ref, a_ref, bb_ref, hm_ref, acc_ref = rest[-5:]
    i = pl.program_id(0)
    j = pl.program_id(1)

    @pl.when(j == 0)
    def _():
        keep = (i % seq_tiles) != 0
        xe_ref[0:_HALO, :] = jnp.where(keep, xp_ref[...], 0.0).astype(BF16)
        xe_ref[_HALO:, :] = x_ref[...].astype(BF16)
        acc_ref[...] = jnp.zeros_like(acc_ref)

    a_ref[...] = _dot(xe_ref[...], wa_ref[...])
    bb_ref[...] = _dot(xe_ref[_HALO:, :], wb_ref[...])
    cw = cw_ref[...]
    cb = cb_ref[...]
    for r0 in range(0, rows, _CONV_ROWS):
        a_0 = a_ref[_HALO + r0:_HALO + r0 + _CONV_ROWS, :]
        a_1 = a_ref[_HALO + r0 - 1:_HALO + r0 - 1 + _CONV_ROWS, :]
        a_2 = a_ref[_HALO + r0 - 2:_HALO + r0 - 2 + _CONV_ROWS, :]
        conv = cw[0:1] * a_2 + cw[1:2] * a_1 + cw[2:3] * a_0 + cb
        hm = _silu(conv) * bb_ref[r0:r0 + _CONV_ROWS, :]
        hm_ref[r0:r0 + _CONV_ROWS, :] = hm.astype(BF16)
    acc_ref[...] += _dot(hm_ref[...], wd_ref[...])

    @pl.when(j == pl.num_programs(1) - 1)
    def _():
        _ln_rows(ln_rows, rows, lambda rs: ALPHA * x_ref[rs, :] + acc_ref[rs, :],
                 g_ref[...], b_ref[...], o_ref, o16_ref)


def _conv_ffn_ln(x2d, seq, layer, w_up, conv_w, conv_b, w_down, g, b, rows, cols, ln_rows,
                 with_bf16):
    m = x2d.shape[0]
    nj = D_FF // cols
    halo_blocks = rows // _HALO
    out_spec = pl.BlockSpec((rows, D_MODEL), lambda i, j: (i, 0))
    out_f32 = jax.ShapeDtypeStruct((m, D_MODEL), F32)
    return pl.pallas_call(
        functools.partial(_ffn_kernel, rows, ln_rows, seq // rows, with_bf16),
        grid=(m // rows, nj),
        in_specs=[
            pl.BlockSpec((rows, D_MODEL), lambda i, j: (i, 0)),
            pl.BlockSpec((_HALO, D_MODEL), lambda i, j: (jnp.maximum(i * halo_blocks - 1, 0), 0)),
            pl.BlockSpec((None, D_MODEL, cols), lambda i, j: (layer, 0, j)),
            pl.BlockSpec((None, D_MODEL, cols), lambda i, j: (layer, 0, nj + j)),
            pl.BlockSpec((CONV_WIDTH, cols), lambda i, j: (0, j)),
            pl.BlockSpec((1, cols), lambda i, j: (0, j)),
            pl.BlockSpec((None, cols, D_MODEL), lambda i, j: (layer, j, 0)),
            pl.BlockSpec((1, D_MODEL), lambda i, j: (0, 0)),
            pl.BlockSpec((1, D_MODEL), lambda i, j: (0, 0)),
        ],
        out_specs=(out_spec, out_spec) if with_bf16 else out_spec,
        out_shape=((out_f32, jax.ShapeDtypeStruct((m, D_MODEL), BF16)) if with_bf16
                   else out_f32),
        scratch_shapes=[
            pltpu.VMEM((_HALO + rows, D_MODEL), BF16),
            pltpu.VMEM((_HALO + rows, cols), F32),
            pltpu.VMEM((rows, cols), F32),
            pltpu.VMEM((rows, cols), BF16),
            pltpu.VMEM((rows, D_MODEL), F32),
        ],
        compiler_params=_params("arbitrary", "arbitrary"),
        name="conv_ffn_ln",
    )(x2d, x2d, w_up, w_up, conv_w.astype(F32), conv_b.astype(F32).reshape(1, D_FF),
      w_down, g.astype(F32).reshape(1, D_MODEL), b.astype(F32).reshape(1, D_MODEL))


def _gmlp_kernel(rows, ln_rows, xc_ref, xn_ref, wu_ref, wv_ref, lng_ref, lnb_ref, ws_ref,
                 bias_ref, o_ref, zv_ref, zu_ref, vn_ref, gate_ref, wc_ref):
    i = pl.program_id(0)
    c = SG_CHUNK
    d = SG_DIM

    @pl.when(i == 0)
    def _():
        causal = (lax.broadcasted_iota(jnp.int32, (c, c), 0)
                  >= lax.broadcasted_iota(jnp.int32, (c, c), 1))
        for g in range(SG_GROUPS):
            wc_ref[g] = jnp.where(causal, ws_ref[g], 0.0).astype(BF16)
        zv_ref[...] = _dot(xc_ref[...], wv_ref[...])

    zu_ref[...] = _dot(xc_ref[...], wu_ref[...])
    lng = lng_ref[...]
    lnb = lnb_ref[...]
    for r0 in range(0, rows, ln_rows):
        rs = slice(r0, r0 + ln_rows)
        vn_ref[rs, :] = _layer_norm(_gelu_exact(zv_ref[rs, :]), lng, lnb).astype(BF16)
    for r0 in range(0, rows, c):
        rs = slice(r0, r0 + c)
        for g in range(SG_GROUPS):
            gs = slice(g * d, (g + 1) * d)
            gate_ref[rs, gs] = _dot(wc_ref[g], vn_ref[rs, gs]) + bias_ref[:, gs]
    zv_ref[...] = _dot(xn_ref[...], wv_ref[...])
    for r0 in range(0, rows, ln_rows):
        rs = slice(r0, r0 + ln_rows)
        o_ref[rs, :] = (_gelu_exact(zu_ref[rs, :]) * gate_ref[rs, :]).astype(BF16)


def _gmlp_mixer(x16, w_in, ln_g, ln_b, w_s, b_s, rows, ln_rows):
    m = x16.shape[0]
    n_tiles = m // rows
    width = SG_GROUPS * SG_DIM
    bias = jnp.broadcast_to(b_s.astype(F32).T[:, :, None], (SG_CHUNK, SG_GROUPS, SG_DIM))
    bias = bias.reshape(SG_CHUNK, width)
    w_in = w_in.astype(BF16)
    once = pl.Buffered(1)
    return pl.pallas_call(
        functools.partial(_gmlp_kernel, rows, ln_rows),
        grid=(n_tiles,),
        in_specs=[
            pl.BlockSpec((rows, D_MODEL), lambda i: (i, 0)),
            pl.BlockSpec((rows, D_MODEL), lambda i: (jnp.minimum(i + 1, n_tiles - 1), 0)),
            pl.BlockSpec((D_MODEL, width), lambda i: (0, 0), pipeline_mode=once),
            pl.BlockSpec((D_MODEL, width), lambda i: (0, 1), pipeline_mode=once),
            pl.BlockSpec((1, width), lambda i: (0, 0)),
            pl.BlockSpec((1, width), lambda i: (0, 0)),
            pl.BlockSpec((SG_GROUPS, SG_CHUNK, SG_CHUNK), lambda i: (0, 0, 0), pipeline_mode=once),
            pl.BlockSpec((SG_CHUNK, width), lambda i: (0, 0), pipeline_mode=once),
        ],
        out_specs=pl.BlockSpec((rows, width), lambda i: (i, 0)),
        out_shape=jax.ShapeDtypeStruct((m, width), BF16),
        scratch_shapes=[
            pltpu.VMEM((rows, width), F32),
            pltpu.VMEM((rows, width), F32),
            pltpu.VMEM((rows, width), BF16),
            pltpu.VMEM((rows, width), F32),
            pltpu.VMEM((SG_GROUPS, SG_CHUNK, SG_CHUNK), BF16),
        ],
        compiler_params=_params("arbitrary"),
        name="gmlp_mixer",
    )(x16, x16, w_in, w_in, ln_g.astype(F32).reshape(1, width),
      ln_b.astype(F32).reshape(1, width), w_s.astype(F32), bias)


def kernel(x, lb_logits, hg_w_in, hg_norm_g, hg_w_out, sg_w_in, sg_ln_g, sg_ln_b, sg_w_s,
           sg_b_s, sg_w_out, ffn_w_up, ffn_conv_w, ffn_conv_b, ffn_w_down, ln1_g, ln1_b,
           ln2_g, ln2_b):
    batch, seq, _ = x.shape
    t = _tiles(batch, seq)
    h = x.astype(F32).reshape(batch * seq, D_MODEL)
    h16 = None
    w_up16 = ffn_w_up.astype(BF16)
    w_down16 = ffn_w_down.astype(BF16)
    for layer in range(DEPTH):
        occ = layer // N_MIXERS
        if layer % N_MIXERS == 0:
            mixed = _hgrn2_mixer(h, batch, seq, layer, lb_logits, hg_w_in[occ], hg_norm_g[occ],
                                 t["hg_rows"])
            w_out = hg_w_out[occ]
        else:
            mixed = _gmlp_mixer(h16 if h16 is not None else h.astype(BF16), sg_w_in[occ],
                                sg_ln_g[occ], sg_ln_b[occ], sg_w_s[occ], sg_b_s[occ],
                                t["sg_rows"], t["ln_rows"])
            w_out = sg_w_out[occ]
        h = _outproj_ln(mixed, w_out, h, ln1_g[layer], ln1_b[layer], t["out_rows"], t["ln_rows"])
        next_is_gmlp = layer + 1 < DEPTH and (layer + 1) % N_MIXERS == 1
        res = _conv_ffn_ln(h, seq, layer, w_up16, ffn_conv_w[layer], ffn_conv_b[layer],
                           w_down16, ln2_g[layer], ln2_b[layer],
                           t["ffn_rows"], t["ffn_cols"], t["ln_rows"], next_is_gmlp)
        h, h16 = res if next_is_gmlp else (res, None)
    return h.reshape(batch, seq, D_MODEL).astype(x.dtype)
```

```python
import functools

import numpy as np
import jax
import jax.numpy as jnp
from jax import lax
from jax.experimental import pallas as pl
from jax.experimental.pallas import tpu as pltpu

D_MODEL = 2048
DEPTH = 2
N_MIXERS = 2
HG_HEADS = 16
HG_DIM = 128
HG_CHUNK = 64
SG_GROUPS = 16
SG_DIM = 128
SG_CHUNK = 128
D_FF = 5632
CONV_WIDTH = 3
ALPHA = (2 * DEPTH) ** 0.25
LN_EPS = 1e-5
RMS_EPS = 1e-6

F32 = jnp.float32
BF16 = jnp.bfloat16

V7X_VMEM_LIMIT_BYTES = 56 * 1024 * 1024
BF16_SUBLANE_TILE = 16


def _tiles(batch, seq):
    rows = batch * seq
    t = 512
    assert seq % t == 0 and rows % t == 0 and D_FF % 512 == 0
    return dict(hg_rows=2 * t, out_rows=256, ffn_rows=t, ffn_cols=512, sg_rows=t, ln_rows=32)


def _sigmoid_pair(z):
    e = jnp.exp(-jnp.abs(z))
    r = 1.0 / (1.0 + e)
    er = e * r
    pos = z >= 0
    return jnp.where(pos, r, er), jnp.where(pos, er, r)


def _silu(z):
    return z / (1.0 + jnp.exp(-z))


def _gelu_exact(z):
    return 0.5 * z * (1.0 + lax.erf(z * (2.0 ** -0.5)))


def _layer_norm(z, g, b):
    mu = jnp.mean(z, axis=-1, keepdims=True)
    zc = z - mu
    var = jnp.mean(zc * zc, axis=-1, keepdims=True)
    return zc * lax.rsqrt(var + LN_EPS) * g + b


def _dot(a, b):
    return jnp.dot(a, b, preferred_element_type=F32)


def _dot_nt(a, b):
    return lax.dot_general(a, b, (((1,), (1,)), ((), ())), preferred_element_type=F32)


def _dot_tn(a, b):
    return lax.dot_general(a, b, (((0,), (0,)), ((), ())), preferred_element_type=F32)


def _params(*semantics):
    return pltpu.CompilerParams(dimension_semantics=semantics,
                                vmem_limit_bytes=V7X_VMEM_LIMIT_BYTES)


_HG_LEVELS = (32, 16, 8, 4, 2, 1)
_N_LEV = len(_HG_LEVELS)


def _hg_tables():
    c = HG_CHUNK
    e = np.zeros((_N_LEV * c, c), np.float32)
    masks = np.zeros((_N_LEV + 1, c, c), np.float32)
    t = np.arange(c)
    for li, h in enumerate(_HG_LEVELS):
        for tt in range(c):
            pos = tt % (2 * h)
            m = tt - pos + h
            if h == 1:
                continue
            if pos >= h:
                e[li * c + tt, m:tt + 1] = 1.0
            else:
                e[li * c + tt, tt + 1:m] = 1.0
        blk = t // (2 * h)
        upper = (t % (2 * h)) >= h
        masks[li] = (blk[:, None] == blk[None, :]) & upper[:, None] & (~upper)[None, :]
    masks[_N_LEV] = np.eye(c)
    e[(_N_LEV - 1) * c:_N_LEV * c] = np.tril(np.ones((c, c)))
    return e, masks


def _hg_head(layer, rows, proj_ref, lbl, ng, e3, m_ref, st, o_ref, lane0, q_ref, k_ref,
             xs_ref, qf_ref, fill):
    c = HG_CHUNK
    d = HG_DIM
    n_chunks = rows // c
    lrows = [lbl[i:i + 1, :] for i in range(DEPTH + 1)]
    mx = functools.reduce(jnp.maximum, lrows)
    ex = [jnp.exp(r - mx) for r in lrows]
    lb = sum(ex[:layer + 1]) / sum(ex)
    one_m_lb = 1.0 - lb

    def rs(ci):
        return slice(ci * c, (ci + 1) * c)

    def xs(ci, block):
        return xs_ref[ci // 2, block * c:(block + 1) * c, (ci % 2) * d:(ci % 2 + 1) * d]

    fill = iter(fill)

    def emit_fill():
        next(fill)()

    emit_fill()
    for pair in range(n_chunks // 2):
        lfs = []
        for ci in (2 * pair, 2 * pair + 1):
            q = _silu(proj_ref[rs(ci), 0:d])
            q_ref[rs(ci), :] = q
            sig, sig_neg = _sigmoid_pair(proj_ref[rs(ci), d:2 * d])
            forget = lb + one_m_lb * sig
            qf_ref[rs(ci), :] = (q * forget).astype(BF16)
            lfs.append(jnp.log(forget))
            k_ref[rs(ci), :] = one_m_lb * sig_neg
        lf2 = jnp.concatenate(lfs, axis=1)
        hi = lf2.astype(BF16)
        r1 = lf2 - hi.astype(F32)
        mid = r1.astype(BF16)
        lo = (r1 - mid.astype(F32)).astype(BF16)
        xs_ref[pair] = _dot(e3, jnp.concatenate([hi, mid, lo], axis=0))
    for _ in range(_hg_n_fill(rows) - 1):
        emit_fill()
    scores = []
    for ci in range(n_chunks):
        q = q_ref[rs(ci), :].astype(BF16)
        k = k_ref[rs(ci), :].astype(BF16)
        both = _dot_nt(jnp.concatenate([q, qf_ref[rs(ci), :]], axis=0), k)
        sc = both[0:c] * m_ref[_N_LEV] + both[c:2 * c] * m_ref[_N_LEV - 1]
        for li in range(_N_LEV - 1):
            w = jnp.exp(xs(ci, li)).astype(BF16)
            sc = sc + _dot_nt(q * w, k * w) * m_ref[li]
        scores.append(sc.astype(BF16))
    intra, kvs, qhs, decays = [], [], [], []
    for ci in range(n_chunks):
        v_b = proj_ref[rs(ci), 2 * d:3 * d].astype(BF16)
        cum = xs(ci, _N_LEV - 1)
        k_hat = k_ref[rs(ci), :] * jnp.exp(cum[c - 1:c, :] - cum)
        kvs.append(_dot_tn(v_b, k_hat.astype(BF16)))
        intra.append(_dot(scores[ci], v_b))
        qhs.append((q_ref[rs(ci), :] * jnp.exp(cum)).astype(BF16))
        decays.append(jnp.exp(cum[c - 1:c, :]))
    outs = []
    for ci in range(n_chunks):
        outs.append(intra[ci] + _dot_nt(qhs[ci], st.astype(BF16)))
        st = st * decays[ci] + kvs[ci]
    for ci in range(n_chunks):
        out = outs[ci]
        out = out * lax.rsqrt(jnp.mean(out * out, axis=-1, keepdims=True) + RMS_EPS)
        o_ref[rs(ci), lane0:lane0 + d] = (
            (out * ng * _silu(proj_ref[rs(ci), 3 * d:4 * d])).astype(BF16))
    return st


def _hg_n_fill(rows):
    return 4


def _proj_pieces(xb_ref, w_refs, proj_ref, n_pieces):
    kdim, d = w_refs[0].shape
    n_split = 2
    k_split = n_pieces // n_split
    kw = kdim // k_split
    pieces = []
    for ni in range(n_split):
        ns = slice(ni * 2 * d, (ni + 1) * 2 * d)
        for ki in range(k_split):
            ks = slice(ki * kw, (ki + 1) * kw)

            def piece(ni=ni, ns=ns, ks=ks, first=(ki == 0)):
                w = jnp.concatenate([w_refs[2 * ni][ks, :], w_refs[2 * ni + 1][ks, :]], axis=1)
                part = _dot(xb_ref[:, ks], w)
                if first:
                    proj_ref[:, ns] = part
                else:
                    proj_ref[:, ns] += part

            pieces.append(piece)
    return pieces


def _hgrn2_kernel(layer, rows, x_ref, *refs):
    w0_refs, wodd_refs, wnext_refs = refs[0:4], refs[4:8], refs[8:12]
    (lbl_ref, ng_ref, e_ref, m_ref, o_ref, xb_ref, pa_ref, pb_ref, st_ref, qa_ref, ka_ref,
     xa_ref, qfa_ref, qb_ref, kb_ref, xsb_ref, qfb_ref) = refs[12:]
    s = pl.program_id(1)
    p = pl.program_id(2)
    d = HG_DIM

    @pl.when(p == 0)
    def _():
        xb_ref[...] = x_ref[...].astype(BF16)
        pa_ref[...] = _dot(xb_ref[...], jnp.concatenate([r[...] for r in w0_refs], axis=1))

    @pl.when(s == 0)
    def _():
        st_ref[2 * p] = jnp.zeros((d, d), F32)
        st_ref[2 * p + 1] = jnp.zeros((d, d), F32)

    e3 = e_ref[...]
    st_ref[2 * p] = _hg_head(layer, rows, pa_ref, lbl_ref[0], ng_ref[0], e3, m_ref,
                             st_ref[2 * p], o_ref, 0, qa_ref, ka_ref, xa_ref, qfa_ref,
                             _proj_pieces(xb_ref, wodd_refs, pb_ref, _hg_n_fill(rows)))
    st_ref[2 * p + 1] = _hg_head(layer, rows, pb_ref, lbl_ref[1], ng_ref[1], e3, m_ref,
                                 st_ref[2 * p + 1], o_ref, d, qb_ref, kb_ref, xsb_ref, qfb_ref,
                                 _proj_pieces(xb_ref, wnext_refs, pa_ref, _hg_n_fill(rows)))


def _hgrn2_mixer(x2d, batch, seq, layer, lb_logits, w_in, norm_g, rows):
    m = x2d.shape[0]
    nst = seq // rows
    d = HG_DIM
    pairs = HG_HEADS // 2
    w = w_in.astype(BF16)

    def head_cols(head_of):
        return [pl.BlockSpec((D_MODEL, d),
                             lambda b, s, p, sec=sec: (0, sec * HG_HEADS + head_of(p)))
                for sec in range(4)]

    lbl = lb_logits.astype(F32).reshape(DEPTH + 1, HG_HEADS, d).transpose(1, 0, 2)
    ng = norm_g.astype(F32).reshape(HG_HEADS, 1, d)
    e_np, m_np = _hg_tables()
    e_all = jnp.asarray(np.concatenate([e_np, e_np, e_np], axis=1), BF16)
    masks = jnp.asarray(m_np, F32)
    return pl.pallas_call(
        functools.partial(_hgrn2_kernel, layer, rows),
        grid=(batch, nst, pairs),
        in_specs=[
            pl.BlockSpec((rows, D_MODEL), lambda b, s, p: (b * nst + s, 0)),
            *head_cols(lambda p: 0),
            *head_cols(lambda p: 2 * p + 1),
            *head_cols(lambda p: jnp.minimum(2 * p + 2, HG_HEADS - 1)),
            pl.BlockSpec((2, DEPTH + 1, d), lambda b, s, p: (p, 0, 0)),
            pl.BlockSpec((2, 1, d), lambda b, s, p: (p, 0, 0)),
            pl.BlockSpec(e_all.shape, lambda b, s, p: (0, 0)),
            pl.BlockSpec(masks.shape, lambda b, s, p: (0, 0, 0)),
        ],
        out_specs=pl.BlockSpec((rows, 2 * d), lambda b, s, p: (b * nst + s, p)),
        out_shape=jax.ShapeDtypeStruct((m, HG_HEADS * d), BF16),
        scratch_shapes=[
            pltpu.VMEM((rows, D_MODEL), BF16),
            pltpu.VMEM((rows, 4 * d), F32),
            pltpu.VMEM((rows, 4 * d), F32),
            pltpu.VMEM((HG_HEADS, d, d), F32),
        ] + 2 * [
            pltpu.VMEM((rows, d), F32),
            pltpu.VMEM((rows, d), F32),
            pltpu.VMEM((rows // (2 * HG_CHUNK), _N_LEV * HG_CHUNK, 2 * d), F32),
            pltpu.VMEM((rows, d), BF16),
        ],
        compiler_params=_params("arbitrary", "arbitrary", "arbitrary"),
        name="hgrn2_mixer",
    )(x2d, *(12 * [w]), lbl, ng, e_all, masks)


_LN_UNROLL = 4


def _ln_rows(ln_rows, n_rows, z_of, g, b, o_ref, o16_ref=None):
    def body(r, carry):
        rs = pl.ds(pl.multiple_of(r * ln_rows, ln_rows), ln_rows)
        y = _layer_norm(z_of(rs), g, b)
        o_ref[rs, :] = y
        if o16_ref is not None:
            o16_ref[rs, :] = y.astype(BF16)
        return carry

    lax.fori_loop(0, n_rows // ln_rows, body, 0, unroll=_LN_UNROLL)


def _outproj_ln_kernel(rows, ln_rows, y0_ref, yodd_ref, ynext_ref, w_ref, x_ref, g_ref, b_ref,
                       o_ref, acca_ref, accb_ref):
    t = pl.program_id(0)

    @pl.when(t == 0)
    def _():
        acca_ref[...] = _dot(y0_ref[...], w_ref[...])

    g = g_ref[...]
    b = b_ref[...]

    def ln(acc_ref, row0):
        for r0 in range(0, rows, ln_rows):
            xs = slice(row0 + r0, row0 + r0 + ln_rows)
            z = ALPHA * x_ref[xs, :] + acc_ref[r0:r0 + ln_rows, :]
            o_ref[xs, :] = _layer_norm(z, g, b)

    accb_ref[...] = _dot(yodd_ref[...], w_ref[...])
    ln(acca_ref, 0)
    acca_ref[...] = _dot(ynext_ref[...], w_ref[...])
    ln(accb_ref, rows)


def _outproj_ln(y2d, w_out, x2d, g, b, rows, ln_rows):
    m = x2d.shape[0]
    n_sub = m // rows
    once = pl.Buffered(1)
    return pl.pallas_call(
        functools.partial(_outproj_ln_kernel, rows, ln_rows),
        grid=(n_sub // 2,),
        in_specs=[
            pl.BlockSpec((rows, D_MODEL), lambda t: (0, 0), pipeline_mode=once),
            pl.BlockSpec((rows, D_MODEL), lambda t: (2 * t + 1, 0)),
            pl.BlockSpec((rows, D_MODEL), lambda t: (jnp.minimum(2 * t + 2, n_sub - 1), 0)),
            pl.BlockSpec((D_MODEL, D_MODEL), lambda t: (0, 0), pipeline_mode=once),
            pl.BlockSpec((2 * rows, D_MODEL), lambda t: (t, 0)),
            pl.BlockSpec((1, D_MODEL), lambda t: (0, 0)),
            pl.BlockSpec((1, D_MODEL), lambda t: (0, 0)),
        ],
        out_specs=pl.BlockSpec((2 * rows, D_MODEL), lambda t: (t, 0)),
        out_shape=jax.ShapeDtypeStruct((m, D_MODEL), F32),
        scratch_shapes=[pltpu.VMEM((rows, D_MODEL), F32), pltpu.VMEM((rows, D_MODEL), F32)],
        compiler_params=_params("arbitrary"),
        name="outproj_ln",
    )(y2d, y2d, y2d, w_out.astype(BF16), x2d, g.astype(F32).reshape(1, D_MODEL),
      b.astype(F32).reshape(1, D_MODEL))


_HALO = BF16_SUBLANE_TILE
_CONV_ROWS = 64


def _ffn_kernel(rows, ln_rows, seq_tiles, with_bf16, x_ref, xp_ref, wa_ref, wb_ref, cw_ref,
                cb_ref, wd_ref, g_ref, b_ref, o_ref, *rest):
    o16_ref = rest[0] if with_bf16 else None
    xe_ref, a_ref, bb_ref, hm_ref, acc_ref = rest[-5:]
    i = pl.program_id(0)
    j = pl.program_id(1)

    @pl.when(j == 0)
    def _():
        keep = (i % seq_tiles) != 0
        xe_ref[0:_HALO, :] = jnp.where(keep, xp_ref[...], 0.0).astype(BF16)
        xe_ref[_HALO:, :] = x_ref[...].astype(BF16)
        acc_ref[...] = jnp.zeros_like(acc_ref)

    a_ref[...] = _dot(xe_ref[...], wa_ref[...])
    bb_ref[...] = _dot(xe_ref[_HALO:, :], wb_ref[...])
    cw = cw_ref[...]
    cb = cb_ref[...]
    for r0 in range(0, rows, _CONV_ROWS):
        a_0 = a_ref[_HALO + r0:_HALO + r0 + _CONV_ROWS, :]
        a_1 = a_ref[_HALO + r0 - 1:_HALO + r0 - 1 + _CONV_ROWS, :]
        a_2 = a_ref[_HALO + r0 - 2:_HALO + r0 - 2 + _CONV_ROWS, :]
        conv = cw[0:1] * a_2 + cw[1:2] * a_1 + cw[2:3] * a_0 + cb
        hm = _silu(conv) * bb_ref[r0:r0 + _CONV_ROWS, :]
        hm_ref[r0:r0 + _CONV_ROWS, :] = hm.astype(BF16)
    acc_ref[...] += _dot(hm_ref[...], wd_ref[...])

    @pl.when(j == pl.num_programs(1) - 1)
    def _():
        _ln_rows(ln_rows, rows, lambda rs: ALPHA * x_ref[rs, :] + acc_ref[rs, :],
                 g_ref[...], b_ref[...], o_ref, o16_ref)


def _conv_ffn_ln(x2d, seq, layer, w_up, conv_w, conv_b, w_down, g, b, rows, cols, ln_rows,
                 with_bf16):
    m = x2d.shape[0]
    nj = D_FF // cols
    halo_blocks = rows // _HALO
    out_spec = pl.BlockSpec((rows, D_MODEL), lambda i, j: (i, 0))
    out_f32 = jax.ShapeDtypeStruct((m, D_MODEL), F32)
    return pl.pallas_call(
        functools.partial(_ffn_kernel, rows, ln_rows, seq // rows, with_bf16),
        grid=(m // rows, nj),
        in_specs=[
            pl.BlockSpec((rows, D_MODEL), lambda i, j: (i, 0)),
            pl.BlockSpec((_HALO, D_MODEL), lambda i, j: (jnp.maximum(i * halo_blocks - 1, 0), 0)),
            pl.BlockSpec((None, D_MODEL, cols), lambda i, j: (layer, 0, j)),
            pl.BlockSpec((None, D_MODEL, cols), lambda i, j: (layer, 0, nj + j)),
            pl.BlockSpec((CONV_WIDTH, cols), lambda i, j: (0, j)),
            pl.BlockSpec((1, cols), lambda i, j: (0, j)),
            pl.BlockSpec((None, cols, D_MODEL), lambda i, j: (layer, j, 0)),
            pl.BlockSpec((1, D_MODEL), lambda i, j: (0, 0)),
            pl.BlockSpec((1, D_MODEL), lambda i, j: (0, 0)),
        ],
        out_specs=(out_spec, out_spec) if with_bf16 else out_spec,
        out_shape=((out_f32, jax.ShapeDtypeStruct((m, D_MODEL), BF16)) if with_bf16
                   else out_f32),
        scratch_shapes=[
            pltpu.VMEM((_HALO + rows, D_MODEL), BF16),
            pltpu.VMEM((_HALO + rows, cols), F32),
            pltpu.VMEM((rows, cols), F32),
            pltpu.VMEM((rows, cols), BF16),
            pltpu.VMEM((rows, D_MODEL), F32),
        ],
        compiler_params=_params("arbitrary", "arbitrary"),
        name="conv_ffn_ln",
    )(x2d, x2d, w_up, w_up, conv_w.astype(F32), conv_b.astype(F32).reshape(1, D_FF),
      w_down, g.astype(F32).reshape(1, D_MODEL), b.astype(F32).reshape(1, D_MODEL))


def _gmlp_kernel(rows, ln_rows, xc_ref, xn_ref, wu_ref, wv_ref, lng_ref, lnb_ref, ws_ref,
                 bias_ref, o_ref, zv_ref, zu_ref, vn_ref, gate_ref, wc_ref):
    i = pl.program_id(0)
    c = SG_CHUNK
    d = SG_DIM

    @pl.when(i == 0)
    def _():
        causal = (lax.broadcasted_iota(jnp.int32, (c, c), 0)
                  >= lax.broadcasted_iota(jnp.int32, (c, c), 1))
        for g in range(SG_GROUPS):
            wc_ref[g] = jnp.where(causal, ws_ref[g], 0.0).astype(BF16)
        zv_ref[...] = _dot(xc_ref[...], wv_ref[...])

    zu_ref[...] = _dot(xc_ref[...], wu_ref[...])
    lng = lng_ref[...]
    lnb = lnb_ref[...]
    for r0 in range(0, rows, ln_rows):
        rs = slice(r0, r0 + ln_rows)
        vn_ref[rs, :] = _layer_norm(_gelu_exact(zv_ref[rs, :]), lng, lnb).astype(BF16)
    for r0 in range(0, rows, c):
        rs = slice(r0, r0 + c)
        for g in range(SG_GROUPS):
            gs = slice(g * d, (g + 1) * d)
            gate_ref[rs, gs] = _dot(wc_ref[g], vn_ref[rs, gs]) + bias_ref[:, gs]
    zv_ref[...] = _dot(xn_ref[...], wv_ref[...])
    for r0 in range(0, rows, ln_rows):
        rs = slice(r0, r0 + ln_rows)
        o_ref[rs, :] = (_gelu_exact(zu_ref[rs, :]) * gate_ref[rs, :]).astype(BF16)


def _gmlp_mixer(x16, w_in, ln_g, ln_b, w_s, b_s, rows, ln_rows):
    m = x16.shape[0]
    n_tiles = m // rows
    width = SG_GROUPS * SG_DIM
    bias = jnp.broadcast_to(b_s.astype(F32).T[:, :, None], (SG_CHUNK, SG_GROUPS, SG_DIM))
    bias = bias.reshape(SG_CHUNK, width)
    w_in = w_in.astype(BF16)
    once = pl.Buffered(1)
    return pl.pallas_call(
        functools.partial(_gmlp_kernel, rows, ln_rows),
        grid=(n_tiles,),
        in_specs=[
            pl.BlockSpec((rows, D_MODEL), lambda i: (i, 0)),
            pl.BlockSpec((rows, D_MODEL), lambda i: (jnp.minimum(i + 1, n_tiles - 1), 0)),
            pl.BlockSpec((D_MODEL, width), lambda i: (0, 0), pipeline_mode=once),
            pl.BlockSpec((D_MODEL, width), lambda i: (0, 1), pipeline_mode=once),
            pl.BlockSpec((1, width), lambda i: (0, 0)),
            pl.BlockSpec((1, width), lambda i: (0, 0)),
            pl.BlockSpec((SG_GROUPS, SG_CHUNK, SG_CHUNK), lambda i: (0, 0, 0), pipeline_mode=once),
            pl.BlockSpec((SG_CHUNK, width), lambda i: (0, 0), pipeline_mode=once),
        ],
        out_specs=pl.BlockSpec((rows, width), lambda i: (i, 0)),
        out_shape=jax.ShapeDtypeStruct((m, width), BF16),
        scratch_shapes=[
            pltpu.VMEM((rows, width), F32),
            pltpu.VMEM((rows, width), F32),
            pltpu.VMEM((rows, width), BF16),
            pltpu.VMEM((rows, width), F32),
            pltpu.VMEM((SG_GROUPS, SG_CHUNK, SG_CHUNK), BF16),
        ],
        compiler_params=_params("arbitrary"),
        name="gmlp_mixer",
    )(x16, x16, w_in, w_in, ln_g.astype(F32).reshape(1, width),
      ln_b.astype(F32).reshape(1, width), w_s.astype(F32), bias)


def kernel(x, lb_logits, hg_w_in, hg_norm_g, hg_w_out, sg_w_in, sg_ln_g, sg_ln_b, sg_w_s,
           sg_b_s, sg_w_out, ffn_w_up, ffn_conv_w, ffn_conv_b, ffn_w_down, ln1_g, ln1_b,
           ln2_g, ln2_b):
    batch, seq, _ = x.shape
    t = _tiles(batch, seq)
    h = x.astype(F32).reshape(batch * seq, D_MODEL)
    h16 = None
    w_up16 = ffn_w_up.astype(BF16)
    w_down16 = ffn_w_down.astype(BF16)
    for layer in range(DEPTH):
        occ = layer // N_MIXERS
        if layer % N_MIXERS == 0:
            mixed = _hgrn2_mixer(h, batch, seq, layer, lb_logits, hg_w_in[occ], hg_norm_g[occ],
                                 t["hg_rows"])
            w_out = hg_w_out[occ]
        else:
            mixed = _gmlp_mixer(h16 if h16 is not None else h.astype(BF16), sg_w_in[occ],
                                sg_ln_g[occ], sg_ln_b[occ], sg_w_s[occ], sg_b_s[occ],
                                t["sg_rows"], t["ln_rows"])
            w_out = sg_w_out[occ]
        h = _outproj_ln(mixed, w_out, h, ln1_g[layer], ln1_b[layer], t["out_rows"], t["ln_rows"])
        next_is_gmlp = layer + 1 < DEPTH and (layer + 1) % N_MIXERS == 1
        res = _conv_ffn_ln(h, seq, layer, w_up16, ffn_conv_w[layer], ffn_conv_b[layer],
                           w_down16, ln2_g[layer], ln2_b[layer],
                           t["ffn_rows"], t["ffn_cols"], t["ln_rows"], next_is_gmlp)
        h, h16 = res if next_is_gmlp else (res, None)
    return h.reshape(batch, seq, D_MODEL).astype(x.dtype)
```

```python
import functools

import numpy as np
import jax
import jax.numpy as jnp
from jax import lax
from jax.experimental import pallas as pl
from jax.experimental.pallas import tpu as pltpu

D_MODEL = 2048
DEPTH = 2
N_MIXERS = 2
HG_HEADS = 16
HG_DIM = 128
HG_CHUNK = 64
SG_GROUPS = 16
SG_DIM = 128
SG_CHUNK = 128
D_FF = 5632
CONV_WIDTH = 3
ALPHA = (2 * DEPTH) ** 0.25
LN_EPS = 1e-5
RMS_EPS = 1e-6

F32 = jnp.float32
BF16 = jnp.bfloat16

V7X_VMEM_LIMIT_BYTES = 56 * 1024 * 1024
BF16_SUBLANE_TILE = 16


def _tiles(batch, seq):
    rows = batch * seq
    t = 512
    assert seq % t == 0 and rows % t == 0 and D_FF % 512 == 0
    return dict(hg_rows=2 * t, out_rows=256, ffn_rows=t, ffn_cols=512, sg_rows=t, ln_rows=32)


def _sigmoid_pair(z):
    e = jnp.exp(-jnp.abs(z))
    r = 1.0 / (1.0 + e)
    er = e * r
    pos = z >= 0
    return jnp.where(pos, r, er), jnp.where(pos, er, r)


def _silu(z):
    return z / (1.0 + jnp.exp(-z))


def _gelu_exact(z):
    return 0.5 * z * (1.0 + lax.erf(z * (2.0 ** -0.5)))


def _layer_norm(z, g, b):
    mu = jnp.mean(z, axis=-1, keepdims=True)
    zc = z - mu
    var = jnp.mean(zc * zc, axis=-1, keepdims=True)
    return zc * lax.rsqrt(var + LN_EPS) * g + b


def _dot(a, b):
    return jnp.dot(a, b, preferred_element_type=F32)


def _dot_nt(a, b):
    return lax.dot_general(a, b, (((1,), (1,)), ((), ())), preferred_element_type=F32)


def _dot_tn(a, b):
    return lax.dot_general(a, b, (((0,), (0,)), ((), ())), preferred_element_type=F32)


def _params(*semantics):
    return pltpu.CompilerParams(dimension_semantics=semantics,
                                vmem_limit_bytes=V7X_VMEM_LIMIT_BYTES)


_HG_LEVELS = (32, 16, 8, 4, 2, 1)
_N_LEV = len(_HG_LEVELS)


def _hg_tables():
    c = HG_CHUNK
    e = np.zeros((_N_LEV * c, c), np.float32)
    masks = np.zeros((_N_LEV + 1, c, c), np.float32)
    t = np.arange(c)
    for li, h in enumerate(_HG_LEVELS):
        for tt in range(c):
            pos = tt % (2 * h)
            m = tt - pos + h
            if h == 1:
                continue
            if pos >= h:
                e[li * c + tt, m:tt + 1] = 1.0
            else:
                e[li * c + tt, tt + 1:m] = 1.0
        blk = t // (2 * h)
        upper = (t % (2 * h)) >= h
        masks[li] = (blk[:, None] == blk[None, :]) & upper[:, None] & (~upper)[None, :]
    masks[_N_LEV] = np.eye(c)
    e[(_N_LEV - 1) * c:_N_LEV * c] = np.tril(np.ones((c, c)))
    return e, masks


def _hg_head(layer, rows, proj_ref, lbl, ng, e3, m_ref, st, o_ref, lane0, q_ref, k_ref,
             xs_ref, qf_ref, fill):
    c = HG_CHUNK
    d = HG_DIM
    n_chunks = rows // c
    lrows = [lbl[i:i + 1, :] for i in range(DEPTH + 1)]
    mx = functools.reduce(jnp.maximum, lrows)
    ex = [jnp.exp(r - mx) for r in lrows]
    lb = sum(ex[:layer + 1]) / sum(ex)
    one_m_lb = 1.0 - lb

    def rs(ci):
        return slice(ci * c, (ci + 1) * c)

    def xs(ci, block):
        return xs_ref[ci // 2, block * c:(block + 1) * c, (ci % 2) * d:(ci % 2 + 1) * d]

    fill = iter(fill)

    def emit_fill():
        next(fill)()

    emit_fill()
    for pair in range(n_chunks // 2):
        lfs = []
        for ci in (2 * pair, 2 * pair + 1):
            q = _silu(proj_ref[rs(ci), 0:d])
            q_ref[rs(ci), :] = q
            sig, sig_neg = _sigmoid_pair(proj_ref[rs(ci), d:2 * d])
            forget = lb + one_m_lb * sig
            qf_ref[rs(ci), :] = (q * forget).astype(BF16)
            lfs.append(jnp.log(forget))
            k_ref[rs(ci), :] = one_m_lb * sig_neg
        lf2 = jnp.concatenate(lfs, axis=1)
        hi = lf2.astype(BF16)
        r1 = lf2 - hi.astype(F32)
        mid = r1.astype(BF16)
        lo = (r1 - mid.astype(F32)).astype(BF16)
        xs_ref[pair] = _dot(e3, jnp.concatenate([hi, mid, lo], axis=0))
    for _ in range(_hg_n_fill(rows) - 1):
        emit_fill()
    scores = []
    for ci in range(n_chunks):
        q = q_ref[rs(ci), :].astype(BF16)
        k = k_ref[rs(ci), :].astype(BF16)
        both = _dot_nt(jnp.concatenate([q, qf_ref[rs(ci), :]], axis=0), k)
        sc = both[0:c] * m_ref[_N_LEV] + both[c:2 * c] * m_ref[_N_LEV - 1]
        for li in range(_N_LEV - 1):
            w = jnp.exp(xs(ci, li)).astype(BF16)
            sc = sc + _dot_nt(q * w, k * w) * m_ref[li]
        scores.append(sc.astype(BF16))
    intra, kvs, qhs, decays = [], [], [], []
    for ci in range(n_chunks):
        v_b = proj_ref[rs(ci), 2 * d:3 * d].astype(BF16)
        cum = xs(ci, _N_LEV - 1)
        k_hat = k_ref[rs(ci), :] * jnp.exp(cum[c - 1:c, :] - cum)
        kvs.append(_dot_tn(v_b, k_hat.astype(BF16)))
        intra.append(_dot(scores[ci], v_b))
        qhs.append((q_ref[rs(ci), :] * jnp.exp(cum)).astype(BF16))
        decays.append(jnp.exp(cum[c - 1:c, :]))
    outs = []
    for ci in range(n_chunks):
        outs.append(intra[ci] + _dot_nt(qhs[ci], st.astype(BF16)))
        st = st * decays[ci] + kvs[ci]
    for ci in range(n_chunks):
        out = outs[ci]
        out = out * lax.rsqrt(jnp.mean(out * out, axis=-1, keepdims=True) + RMS_EPS)
        o_ref[rs(ci), lane0:lane0 + d] = (
            (out * ng * _silu(proj_ref[rs(ci), 3 * d:4 * d])).astype(BF16))
    return st


def _hg_n_fill(rows):
    return 4


def _proj_pieces(xb_ref, w_refs, proj_ref, n_pieces):
    kdim, d = w_refs[0].shape
    n_split = 2
    k_split = n_pieces // n_split
    kw = kdim // k_split
    pieces = []
    for ni in range(n_split):
        ns = slice(ni * 2 * d, (ni + 1) * 2 * d)
        for ki in range(k_split):
            ks = slice(ki * kw, (ki + 1) * kw)

            def piece(ni=ni, ns=ns, ks=ks, first=(ki == 0)):
                w = jnp.concatenate([w_refs[2 * ni][ks, :], w_refs[2 * ni + 1][ks, :]], axis=1)
                part = _dot(xb_ref[:, ks], w)
                if first:
                    proj_ref[:, ns] = part
                else:
                    proj_ref[:, ns] += part

            pieces.append(piece)
    return pieces


def _hgrn2_kernel(layer, rows, x_ref, *refs):
    w0_refs, wodd_refs, wnext_refs = refs[0:4], refs[4:8], refs[8:12]
    (lbl_ref, ng_ref, e_ref, m_ref, o_ref, xb_ref, pa_ref, pb_ref, st_ref, qa_ref, ka_ref,
     xa_ref, qfa_ref, qb_ref, kb_ref, xsb_ref, qfb_ref) = refs[12:]
    s = pl.program_id(1)
    p = pl.program_id(2)
    d = HG_DIM

    @pl.when(p == 0)
    def _():
        xb_ref[...] = x_ref[...].astype(BF16)
        pa_ref[...] = _dot(xb_ref[...], jnp.concatenate([r[...] for r in w0_refs], axis=1))

    @pl.when(s == 0)
    def _():
        st_ref[2 * p] = jnp.zeros((d, d), F32)
        st_ref[2 * p + 1] = jnp.zeros((d, d), F32)

    e3 = e_ref[...]
    st_ref[2 * p] = _hg_head(layer, rows, pa_ref, lbl_ref[0], ng_ref[0], e3, m_ref,
                             st_ref[2 * p], o_ref, 0, qa_ref, ka_ref, xa_ref, qfa_ref,
                             _proj_pieces(xb_ref, wodd_refs, pb_ref, _hg_n_fill(rows)))
    st_ref[2 * p + 1] = _hg_head(layer, rows, pb_ref, lbl_ref[1], ng_ref[1], e3, m_ref,
                                 st_ref[2 * p + 1], o_ref, d, qb_ref, kb_ref, xsb_ref, qfb_ref,
                                 _proj_pieces(xb_ref, wnext_refs, pa_ref, _hg_n_fill(rows)))


def _hgrn2_mixer(x2d, batch, seq, layer, lb_logits, w_in, norm_g, rows):
    m = x2d.shape[0]
    nst = seq // rows
    d = HG_DIM
    pairs = HG_HEADS // 2
    w = w_in.astype(BF16)

    def head_cols(head_of):
        return [pl.BlockSpec((D_MODEL, d),
                             lambda b, s, p, sec=sec: (0, sec * HG_HEADS + head_of(p)))
                for sec in range(4)]

    lbl = lb_logits.astype(F32).reshape(DEPTH + 1, HG_HEADS, d).transpose(1, 0, 2)
    ng = norm_g.astype(F32).reshape(HG_HEADS, 1, d)
    e_np, m_np = _hg_tables()
    e_all = jnp.asarray(np.concatenate([e_np, e_np, e_np], axis=1), BF16)
    masks = jnp.asarray(m_np, F32)
    return pl.pallas_call(
        functools.partial(_hgrn2_kernel, layer, rows),
        grid=(batch, nst, pairs),
        in_specs=[
            pl.BlockSpec((rows, D_MODEL), lambda b, s, p: (b * nst + s, 0)),
            *head_cols(lambda p: 0),
            *head_cols(lambda p: 2 * p + 1),
            *head_cols(lambda p: jnp.minimum(2 * p + 2, HG_HEADS - 1)),
            pl.BlockSpec((2, DEPTH + 1, d), lambda b, s, p: (p, 0, 0)),
            pl.BlockSpec((2, 1, d), lambda b, s, p: (p, 0, 0)),
            pl.BlockSpec(e_all.shape, lambda b, s, p: (0, 0)),
            pl.BlockSpec(masks.shape, lambda b, s, p: (0, 0, 0)),
        ],
        out_specs=pl.BlockSpec((rows, 2 * d), lambda b, s, p: (b * nst + s, p)),
        out_shape=jax.ShapeDtypeStruct((m, HG_HEADS * d), BF16),
        scratch_shapes=[
            pltpu.VMEM((rows, D_MODEL), BF16),
            pltpu.VMEM((rows, 4 * d), F32),
            pltpu.VMEM((rows, 4 * d), F32),
            pltpu.VMEM((HG_HEADS, d, d), F32),
        ] + 2 * [
            pltpu.VMEM((rows, d), F32),
            pltpu.VMEM((rows, d), F32),
            pltpu.VMEM((rows // (2 * HG_CHUNK), _N_LEV * HG_CHUNK, 2 * d), F32),
            pltpu.VMEM((rows, d), BF16),
        ],
        compiler_params=_params("arbitrary", "arbitrary", "arbitrary"),
        name="hgrn2_mixer",
    )(x2d, *(12 * [w]), lbl, ng, e_all, masks)


_LN_UNROLL = 4


def _ln_rows(ln_rows, n_rows, z_of, g, b, o_ref, o16_ref=None):
    def body(r, carry):
        rs = pl.ds(pl.multiple_of(r * ln_rows, ln_rows), ln_rows)
        y = _layer_norm(z_of(rs), g, b)
        o_ref[rs, :] = y
        if o16_ref is not None:
            o16_ref[rs, :] = y.astype(BF16)
        return carry

    lax.fori_loop(0, n_rows // ln_rows, body, 0, unroll=_LN_UNROLL)


def _outproj_ln_kernel(rows, ln_rows, y0_ref, yodd_ref, ynext_ref, w_ref, x_ref, g_ref, b_ref,
                       o_ref, acca_ref, accb_ref):
    t = pl.program_id(0)

    @pl.when(t == 0)
    def _():
        acca_ref[...] = _dot(y0_ref[...], w_ref[...])

    g = g_ref[...]
    b = b_ref[...]

    def ln(acc_ref, row0):
        for r0 in range(0, rows, ln_rows):
            xs = slice(row0 + r0, row0 + r0 + ln_rows)
            z = ALPHA * x_ref[xs, :] + acc_ref[r0:r0 + ln_rows, :]
            o_ref[xs, :] = _layer_norm(z, g, b)

    accb_ref[...] = _dot(yodd_ref[...], w_ref[...])
    ln(acca_ref, 0)
    acca_ref[...] = _dot(ynext_ref[...], w_ref[...])
    ln(accb_ref, rows)


def _outproj_ln(y2d, w_out, x2d, g, b, rows, ln_rows):
    m = x2d.shape[0]
    n_sub = m // rows
    once = pl.Buffered(1)
    return pl.pallas_call(
        functools.partial(_outproj_ln_kernel, rows, ln_rows),
        grid=(n_sub // 2,),
        in_specs=[
            pl.BlockSpec((rows, D_MODEL), lambda t: (0, 0), pipeline_mode=once),
            pl.BlockSpec((rows, D_MODEL), lambda t: (2 * t + 1, 0)),
            pl.BlockSpec((rows, D_MODEL), lambda t: (jnp.minimum(2 * t + 2, n_sub - 1), 0)),
            pl.BlockSpec((D_MODEL, D_MODEL), lambda t: (0, 0), pipeline_mode=once),
            pl.BlockSpec((2 * rows, D_MODEL), lambda t: (t, 0)),
            pl.BlockSpec((1, D_MODEL), lambda t: (0, 0)),
            pl.BlockSpec((1, D_MODEL), lambda t: (0, 0)),
        ],
        out_specs=pl.BlockSpec((2 * rows, D_MODEL), lambda t: (t, 0)),
        out_shape=jax.ShapeDtypeStruct((m, D_MODEL), F32),
        scratch_shapes=[pltpu.VMEM((rows, D_MODEL), F32), pltpu.VMEM((rows, D_MODEL), F32)],
        compiler_params=_params("arbitrary"),
        name="outproj_ln",
    )(y2d, y2d, y2d, w_out.astype(BF16), x2d, g.astype(F32).reshape(1, D_MODEL),
      b.astype(F32).reshape(1, D_MODEL))


_HALO = BF16_SUBLANE_TILE
_CONV_ROWS = 64


_W_BUFS = 3


def _ffn_kernel(rows, ln_rows, seq_tiles, cols, layer, with_bf16, x_ref, xp_ref, wup_hbm,
                wdn_hbm, cw_ref, cb_ref, g_ref, b_ref, o_ref, *rest):
    o16_ref = rest[0] if with_bf16 else None
    xe_ref, a_ref, bb_ref, hm_ref, acc_ref, wa_buf, wb_buf, wd_buf, sem = rest[-9:]
    i = pl.program_id(0)
    j = pl.program_id(1)
    nj = pl.num_programs(1)
    n = i * nj + j
    n_steps = pl.num_programs(0) * nj

    def weight_copies(step):
        slot = step % _W_BUFS
        c0 = pl.multiple_of((step % nj) * cols, cols)
        return (
            pltpu.make_async_copy(wup_hbm.at[layer, :, pl.ds(c0, cols)], wa_buf.at[slot],
                                  sem.at[0, slot]),
            pltpu.make_async_copy(wup_hbm.at[layer, :, pl.ds(D_FF + c0, cols)], wb_buf.at[slot],
                                  sem.at[1, slot]),
            pltpu.make_async_copy(wdn_hbm.at[layer, pl.ds(c0, cols), :], wd_buf.at[slot],
                                  sem.at[2, slot]),
        )

    @pl.when(n == 0)
    def _():
        for ahead in range(_W_BUFS - 1):
            for cp in weight_copies(ahead):
                cp.start()

    @pl.when(n + _W_BUFS - 1 < n_steps)
    def _():
        for cp in weight_copies(n + _W_BUFS - 1):
            cp.start()

    slot = n % _W_BUFS
    wa_ref = wa_buf.at[slot]
    wb_ref = wb_buf.at[slot]
    wd_ref = wd_buf.at[slot]

    @pl.when(j == 0)
    def _():
        keep = (i % seq_tiles) != 0
        xe_ref[0:_HALO, :] = jnp.where(keep, xp_ref[...], 0.0).astype(BF16)
        xe_ref[_HALO:, :] = x_ref[...].astype(BF16)
        acc_ref[...] = jnp.zeros_like(acc_ref)

    for cp in weight_copies(n):
        cp.wait()
    a_ref[...] = _dot(xe_ref[...], wa_ref[...])
    bb_ref[...] = _dot(xe_ref[_HALO:, :], wb_ref[...])
    cw = cw_ref[...]
    cb = cb_ref[...]
    for r0 in range(0, rows, _CONV_ROWS):
        a_0 = a_ref[_HALO + r0:_HALO + r0 + _CONV_ROWS, :]
        a_1 = a_ref[_HALO + r0 - 1:_HALO + r0 - 1 + _CONV_ROWS, :]
        a_2 = a_ref[_HALO + r0 - 2:_HALO + r0 - 2 + _CONV_ROWS, :]
        conv = cw[0:1] * a_2 + cw[1:2] * a_1 + cw[2:3] * a_0 + cb
        hm = _silu(conv) * bb_ref[r0:r0 + _CONV_ROWS, :]
        hm_ref[r0:r0 + _CONV_ROWS, :] = hm.astype(BF16)
    acc_ref[...] += _dot(hm_ref[...], wd_ref[...])

    @pl.when(j == pl.num_programs(1) - 1)
    def _():
        _ln_rows(ln_rows, rows, lambda rs: ALPHA * x_ref[rs, :] + acc_ref[rs, :],
                 g_ref[...], b_ref[...], o_ref, o16_ref)


def _conv_ffn_ln(x2d, seq, layer, w_up, conv_w, conv_b, w_down, g, b, rows, cols, ln_rows,
                 with_bf16):
    m = x2d.shape[0]
    nj = D_FF // cols
    halo_blocks = rows // _HALO
    out_spec = pl.BlockSpec((rows, D_MODEL), lambda i, j: (i, 0))
    out_f32 = jax.ShapeDtypeStruct((m, D_MODEL), F32)
    return pl.pallas_call(
        functools.partial(_ffn_kernel, rows, ln_rows, seq // rows, cols, layer, with_bf16),
        grid=(m // rows, nj),
        in_specs=[
            pl.BlockSpec((rows, D_MODEL), lambda i, j: (i, 0)),
            pl.BlockSpec((_HALO, D_MODEL), lambda i, j: (jnp.maximum(i * halo_blocks - 1, 0), 0)),
            pl.BlockSpec(memory_space=pl.ANY),
            pl.BlockSpec(memory_space=pl.ANY),
            pl.BlockSpec((CONV_WIDTH, cols), lambda i, j: (0, j)),
            pl.BlockSpec((1, cols), lambda i, j: (0, j)),
            pl.BlockSpec((1, D_MODEL), lambda i, j: (0, 0)),
            pl.BlockSpec((1, D_MODEL), lambda i, j: (0, 0)),
        ],
        out_specs=(out_spec, out_spec) if with_bf16 else out_spec,
        out_shape=((out_f32, jax.ShapeDtypeStruct((m, D_MODEL), BF16)) if with_bf16
                   else out_f32),
        scratch_shapes=[
            pltpu.VMEM((_HALO + rows, D_MODEL), BF16),
            pltpu.VMEM((_HALO + rows, cols), F32),
            pltpu.VMEM((rows, cols), F32),
            pltpu.VMEM((rows, cols), BF16),
            pltpu.VMEM((rows, D_MODEL), F32),
            pltpu.VMEM((_W_BUFS, D_MODEL, cols), BF16),
            pltpu.VMEM((_W_BUFS, D_MODEL, cols), BF16),
            pltpu.VMEM((_W_BUFS, cols, D_MODEL), BF16),
            pltpu.SemaphoreType.DMA((3, _W_BUFS)),
        ],
        compiler_params=_params("arbitrary", "arbitrary"),
        name="conv_ffn_ln",
    )(x2d, x2d, w_up, w_down, conv_w.astype(F32), conv_b.astype(F32).reshape(1, D_FF),
      g.astype(F32).reshape(1, D_MODEL), b.astype(F32).reshape(1, D_MODEL))


def _gmlp_kernel(rows, ln_rows, xc_ref, xn_ref, wu_ref, wv_ref, lng_ref, lnb_ref, ws_ref,
                 bias_ref, o_ref, zv_ref, zu_ref, vn_ref, gate_ref, wc_ref):
    i = pl.program_id(0)
    c = SG_CHUNK
    d = SG_DIM

    @pl.when(i == 0)
    def _():
        causal = (lax.broadcasted_iota(jnp.int32, (c, c), 0)
                  >= lax.broadcasted_iota(jnp.int32, (c, c), 1))
        for g in range(SG_GROUPS):
            wc_ref[g] = jnp.where(causal, ws_ref[g], 0.0).astype(BF16)
        zv_ref[...] = _dot(xc_ref[...], wv_ref[...])

    zu_ref[...] = _dot(xc_ref[...], wu_ref[...])
    lng = lng_ref[...]
    lnb = lnb_ref[...]
    for r0 in range(0, rows, ln_rows):
        rs = slice(r0, r0 + ln_rows)
        vn_ref[rs, :] = _layer_norm(_gelu_exact(zv_ref[rs, :]), lng, lnb).astype(BF16)
    for r0 in range(0, rows, c):
        rs = slice(r0, r0 + c)
        for g in range(SG_GROUPS):
            gs = slice(g * d, (g + 1) * d)
            gate_ref[rs, gs] = _dot(wc_ref[g], vn_ref[rs, gs]) + bias_ref[:, gs]
    zv_ref[...] = _dot(xn_ref[...], wv_ref[...])
    for r0 in range(0, rows, ln_rows):
        rs = slice(r0, r0 + ln_rows)
        o_ref[rs, :] = (_gelu_exact(zu_ref[rs, :]) * gate_ref[rs, :]).astype(BF16)


def _gmlp_mixer(x16, w_in, ln_g, ln_b, w_s, b_s, rows, ln_rows):
    m = x16.shape[0]
    n_tiles = m // rows
    width = SG_GROUPS * SG_DIM
    bias = jnp.broadcast_to(b_s.astype(F32).T[:, :, None], (SG_CHUNK, SG_GROUPS, SG_DIM))
    bias = bias.reshape(SG_CHUNK, width)
    w_in = w_in.astype(BF16)
    once = pl.Buffered(1)
    return pl.pallas_call(
        functools.partial(_gmlp_kernel, rows, ln_rows),
        grid=(n_tiles,),
        in_specs=[
            pl.BlockSpec((rows, D_MODEL), lambda i: (i, 0)),
            pl.BlockSpec((rows, D_MODEL), lambda i: (jnp.minimum(i + 1, n_tiles - 1), 0)),
            pl.BlockSpec((D_MODEL, width), lambda i: (0, 0), pipeline_mode=once),
            pl.BlockSpec((D_MODEL, width), lambda i: (0, 1), pipeline_mode=once),
            pl.BlockSpec((1, width), lambda i: (0, 0)),
            pl.BlockSpec((1, width), lambda i: (0, 0)),
            pl.BlockSpec((SG_GROUPS, SG_CHUNK, SG_CHUNK), lambda i: (0, 0, 0), pipeline_mode=once),
            pl.BlockSpec((SG_CHUNK, width), lambda i: (0, 0), pipeline_mode=once),
        ],
        out_specs=pl.BlockSpec((rows, width), lambda i: (i, 0)),
        out_shape=jax.ShapeDtypeStruct((m, width), BF16),
        scratch_shapes=[
            pltpu.VMEM((rows, width), F32),
            pltpu.VMEM((rows, width), F32),
            pltpu.VMEM((rows, width), BF16),
            pltpu.VMEM((rows, width), F32),
            pltpu.VMEM((SG_GROUPS, SG_CHUNK, SG_CHUNK), BF16),
        ],
        compiler_params=_params("arbitrary"),
        name="gmlp_mixer",
    )(x16, x16, w_in, w_in, ln_g.astype(F32).reshape(1, width),
      ln_b.astype(F32).reshape(1, width), w_s.astype(F32), bias)


def kernel(x, lb_logits, hg_w_in, hg_norm_g, hg_w_out, sg_w_in, sg_ln_g, sg_ln_b, sg_w_s,
           sg_b_s, sg_w_out, ffn_w_up, ffn_conv_w, ffn_conv_b, ffn_w_down, ln1_g, ln1_b,
           ln2_g, ln2_b):
    batch, seq, _ = x.shape
    t = _tiles(batch, seq)
    h = x.astype(F32).reshape(batch * seq, D_MODEL)
    h16 = None
    w_up16 = ffn_w_up.astype(BF16)
    w_down16 = ffn_w_down.astype(BF16)
    for layer in range(DEPTH):
        occ = layer // N_MIXERS
        if layer % N_MIXERS == 0:
            mixed = _hgrn2_mixer(h, batch, seq, layer, lb_logits, hg_w_in[occ], hg_norm_g[occ],
                                 t["hg_rows"])
            w_out = hg_w_out[occ]
        else:
            mixed = _gmlp_mixer(h16 if h16 is not None else h.astype(BF16), sg_w_in[occ],
                                sg_ln_g[occ], sg_ln_b[occ], sg_w_s[occ], sg_b_s[occ],
                                t["sg_rows"], t["ln_rows"])
            w_out = sg_w_out[occ]
        h = _outproj_ln(mixed, w_out, h, ln1_g[layer], ln1_b[layer], t["out_rows"], t["ln_rows"])
        next_is_gmlp = layer + 1 < DEPTH and (layer + 1) % N_MIXERS == 1
        res = _conv_ffn_ln(h, seq, layer, w_up16, ffn_conv_w[layer], ffn_conv_b[layer],
                           w_down16, ln2_g[layer], ln2_b[layer],
                           t["ffn_rows"], t["ffn_cols"], t["ln_rows"], next_is_gmlp)
        h, h16 = res if next_is_gmlp else (res, None)
    return h.reshape(batch, seq, D_MODEL).astype(x.dtype)
```

```python
import functools

import numpy as np
import jax
import jax.numpy as jnp
from jax import lax
from jax.experimental import pallas as pl
from jax.experimental.pallas import tpu as pltpu

D_MODEL = 2048
DEPTH = 2
N_MIXERS = 2
HG_HEADS = 16
HG_DIM = 128
HG_CHUNK = 64
SG_GROUPS = 16
SG_DIM = 128
SG_CHUNK = 128
D_FF = 5632
CONV_WIDTH = 3
ALPHA = (2 * DEPTH) ** 0.25
LN_EPS = 1e-5
RMS_EPS = 1e-6

F32 = jnp.float32
BF16 = jnp.bfloat16

V7X_VMEM_LIMIT_BYTES = 56 * 1024 * 1024
BF16_SUBLANE_TILE = 16


def _tiles(batch, seq):
    rows = batch * seq
    t = 512
    assert seq % t == 0 and rows % t == 0 and D_FF % 512 == 0
    return dict(hg_rows=2 * t, out_rows=256, ffn_rows=t, ffn_cols=512, sg_rows=t, ln_rows=32)


def _sigmoid_pair(z):
    e = jnp.exp(-jnp.abs(z))
    r = 1.0 / (1.0 + e)
    er = e * r
    pos = z >= 0
    return jnp.where(pos, r, er), jnp.where(pos, er, r)


def _silu(z):
    return z / (1.0 + jnp.exp(-z))


def _gelu_exact(z):
    return 0.5 * z * (1.0 + lax.erf(z * (2.0 ** -0.5)))


def _layer_norm(z, g, b):
    mu = jnp.mean(z, axis=-1, keepdims=True)
    zc = z - mu
    var = jnp.mean(zc * zc, axis=-1, keepdims=True)
    return zc * lax.rsqrt(var + LN_EPS) * g + b


def _dot(a, b):
    return jnp.dot(a, b, preferred_element_type=F32)


def _dot_nt(a, b):
    return lax.dot_general(a, b, (((1,), (1,)), ((), ())), preferred_element_type=F32)


def _dot_tn(a, b):
    return lax.dot_general(a, b, (((0,), (0,)), ((), ())), preferred_element_type=F32)


def _params(*semantics):
    return pltpu.CompilerParams(dimension_semantics=semantics,
                                vmem_limit_bytes=V7X_VMEM_LIMIT_BYTES)


_HG_LEVELS = (32, 16, 8, 4, 2, 1)
_N_LEV = len(_HG_LEVELS)


def _hg_tables():
    c = HG_CHUNK
    e = np.zeros((_N_LEV * c, c), np.float32)
    masks = np.zeros((_N_LEV + 1, c, c), np.float32)
    t = np.arange(c)
    for li, h in enumerate(_HG_LEVELS):
        for tt in range(c):
            pos = tt % (2 * h)
            m = tt - pos + h
            if h == 1:
                continue
            if pos >= h:
                e[li * c + tt, m:tt + 1] = 1.0
            else:
                e[li * c + tt, tt + 1:m] = 1.0
        blk = t // (2 * h)
        upper = (t % (2 * h)) >= h
        masks[li] = (blk[:, None] == blk[None, :]) & upper[:, None] & (~upper)[None, :]
    masks[_N_LEV] = np.eye(c)
    e[(_N_LEV - 1) * c:_N_LEV * c] = np.tril(np.ones((c, c)))
    return e, masks


def _hg_head(layer, rows, proj_ref, lbl, ng, e3, m_ref, st, o_ref, lane0, q_ref, k_ref,
             xs_ref, qf_ref, fill):
    c = HG_CHUNK
    d = HG_DIM
    n_chunks = rows // c
    lrows = [lbl[i:i + 1, :] for i in range(DEPTH + 1)]
    mx = functools.reduce(jnp.maximum, lrows)
    ex = [jnp.exp(r - mx) for r in lrows]
    lb = sum(ex[:layer + 1]) / sum(ex)
    one_m_lb = 1.0 - lb

    def rs(ci):
        return slice(ci * c, (ci + 1) * c)

    def xs(ci, block):
        return xs_ref[ci // 2, block * c:(block + 1) * c, (ci % 2) * d:(ci % 2 + 1) * d]

    fill = iter(fill)

    def emit_fill():
        next(fill)()

    emit_fill()
    for pair in range(n_chunks // 2):
        lfs = []
        for ci in (2 * pair, 2 * pair + 1):
            q = _silu(proj_ref[rs(ci), 0:d])
            q_ref[rs(ci), :] = q
            sig, sig_neg = _sigmoid_pair(proj_ref[rs(ci), d:2 * d])
            forget = lb + one_m_lb * sig
            qf_ref[rs(ci), :] = (q * forget).astype(BF16)
            lfs.append(jnp.log(forget))
            k_ref[rs(ci), :] = one_m_lb * sig_neg
        lf2 = jnp.concatenate(lfs, axis=1)
        hi = lf2.astype(BF16)
        r1 = lf2 - hi.astype(F32)
        mid = r1.astype(BF16)
        lo = (r1 - mid.astype(F32)).astype(BF16)
        xs_ref[pair] = _dot(e3, jnp.concatenate([hi, mid, lo], axis=0))
    for _ in range(_hg_n_fill(rows) - 1):
        emit_fill()
    scores = []
    for ci in range(n_chunks):
        q = q_ref[rs(ci), :].astype(BF16)
        k = k_ref[rs(ci), :].astype(BF16)
        both = _dot_nt(jnp.concatenate([q, qf_ref[rs(ci), :]], axis=0), k)
        sc = both[0:c] * m_ref[_N_LEV] + both[c:2 * c] * m_ref[_N_LEV - 1]
        for li in range(_N_LEV - 1):
            w = jnp.exp(xs(ci, li)).astype(BF16)
            sc = sc + _dot_nt(q * w, k * w) * m_ref[li]
        scores.append(sc.astype(BF16))
    intra, kvs, qhs, decays = [], [], [], []
    for ci in range(n_chunks):
        v_b = proj_ref[rs(ci), 2 * d:3 * d].astype(BF16)
        cum = xs(ci, _N_LEV - 1)
        k_hat = k_ref[rs(ci), :] * jnp.exp(cum[c - 1:c, :] - cum)
        kvs.append(_dot_tn(v_b, k_hat.astype(BF16)))
        intra.append(_dot(scores[ci], v_b))
        qhs.append((q_ref[rs(ci), :] * jnp.exp(cum)).astype(BF16))
        decays.append(jnp.exp(cum[c - 1:c, :]))
    outs = []
    for ci in range(n_chunks):
        outs.append(intra[ci] + _dot_nt(qhs[ci], st.astype(BF16)))
        st = st * decays[ci] + kvs[ci]
    for ci in range(n_chunks):
        out = outs[ci]
        out = out * lax.rsqrt(jnp.mean(out * out, axis=-1, keepdims=True) + RMS_EPS)
        o_ref[rs(ci), lane0:lane0 + d] = (
            (out * ng * _silu(proj_ref[rs(ci), 3 * d:4 * d])).astype(BF16))
    return st


def _hg_n_fill(rows):
    return 4


def _proj_pieces(xb_ref, w_refs, proj_ref, n_pieces):
    kdim, d = w_refs[0].shape
    n_split = 2
    k_split = n_pieces // n_split
    kw = kdim // k_split
    pieces = []
    for ni in range(n_split):
        ns = slice(ni * 2 * d, (ni + 1) * 2 * d)
        for ki in range(k_split):
            ks = slice(ki * kw, (ki + 1) * kw)

            def piece(ni=ni, ns=ns, ks=ks, first=(ki == 0)):
                w = jnp.concatenate([w_refs[2 * ni][ks, :], w_refs[2 * ni + 1][ks, :]], axis=1)
                part = _dot(xb_ref[:, ks], w)
                if first:
                    proj_ref[:, ns] = part
                else:
                    proj_ref[:, ns] += part

            pieces.append(piece)
    return pieces


def _hgrn2_kernel(layer, rows, x_ref, *refs):
    w0_refs, wodd_refs, wnext_refs = refs[0:4], refs[4:8], refs[8:12]
    (lbl_ref, ng_ref, e_ref, m_ref, o_ref, xb_ref, pa_ref, pb_ref, st_ref, qa_ref, ka_ref,
     xa_ref, qfa_ref, qb_ref, kb_ref, xsb_ref, qfb_ref) = refs[12:]
    s = pl.program_id(1)
    p = pl.program_id(2)
    d = HG_DIM

    @pl.when(p == 0)
    def _():
        xb_ref[...] = x_ref[...].astype(BF16)
        pa_ref[...] = _dot(xb_ref[...], jnp.concatenate([r[...] for r in w0_refs], axis=1))

    @pl.when(s == 0)
    def _():
        st_ref[2 * p] = jnp.zeros((d, d), F32)
        st_ref[2 * p + 1] = jnp.zeros((d, d), F32)

    e3 = e_ref[...]
    st_ref[2 * p] = _hg_head(layer, rows, pa_ref, lbl_ref[0], ng_ref[0], e3, m_ref,
                             st_ref[2 * p], o_ref, 0, qa_ref, ka_ref, xa_ref, qfa_ref,
                             _proj_pieces(xb_ref, wodd_refs, pb_ref, _hg_n_fill(rows)))
    st_ref[2 * p + 1] = _hg_head(layer, rows, pb_ref, lbl_ref[1], ng_ref[1], e3, m_ref,
                                 st_ref[2 * p + 1], o_ref, d, qb_ref, kb_ref, xsb_ref, qfb_ref,
                                 _proj_pieces(xb_ref, wnext_refs, pa_ref, _hg_n_fill(rows)))


def _hgrn2_mixer(x2d, batch, seq, layer, lb_logits, w_in, norm_g, rows):
    m = x2d.shape[0]
    nst = seq // rows
    d = HG_DIM
    pairs = HG_HEADS // 2
    w = w_in.astype(BF16)

    def head_cols(head_of):
        return [pl.BlockSpec((D_MODEL, d),
                             lambda b, s, p, sec=sec: (0, sec * HG_HEADS + head_of(p)))
                for sec in range(4)]

    lbl = lb_logits.astype(F32).reshape(DEPTH + 1, HG_HEADS, d).transpose(1, 0, 2)
    ng = norm_g.astype(F32).reshape(HG_HEADS, 1, d)
    e_np, m_np = _hg_tables()
    e_all = jnp.asarray(np.concatenate([e_np, e_np, e_np], axis=1), BF16)
    masks = jnp.asarray(m_np, F32)
    return pl.pallas_call(
        functools.partial(_hgrn2_kernel, layer, rows),
        grid=(batch, nst, pairs),
        in_specs=[
            pl.BlockSpec((rows, D_MODEL), lambda b, s, p: (b * nst + s, 0)),
            *head_cols(lambda p: 0),
            *head_cols(lambda p: 2 * p + 1),
            *head_cols(lambda p: jnp.minimum(2 * p + 2, HG_HEADS - 1)),
            pl.BlockSpec((2, DEPTH + 1, d), lambda b, s, p: (p, 0, 0)),
            pl.BlockSpec((2, 1, d), lambda b, s, p: (p, 0, 0)),
            pl.BlockSpec(e_all.shape, lambda b, s, p: (0, 0)),
            pl.BlockSpec(masks.shape, lambda b, s, p: (0, 0, 0)),
        ],
        out_specs=pl.BlockSpec((rows, 2 * d), lambda b, s, p: (b * nst + s, p)),
        out_shape=jax.ShapeDtypeStruct((m, HG_HEADS * d), BF16),
        scratch_shapes=[
            pltpu.VMEM((rows, D_MODEL), BF16),
            pltpu.VMEM((rows, 4 * d), F32),
            pltpu.VMEM((rows, 4 * d), F32),
            pltpu.VMEM((HG_HEADS, d, d), F32),
        ] + 2 * [
            pltpu.VMEM((rows, d), F32),
            pltpu.VMEM((rows, d), F32),
            pltpu.VMEM((rows // (2 * HG_CHUNK), _N_LEV * HG_CHUNK, 2 * d), F32),
            pltpu.VMEM((rows, d), BF16),
        ],
        compiler_params=_params("arbitrary", "arbitrary", "arbitrary"),
        name="hgrn2_mixer",
    )(x2d, *(12 * [w]), lbl, ng, e_all, masks)


_LN_UNROLL = 4


def _ln_rows(ln_rows, n_rows, z_of, g, b, o_ref, o16_ref=None):
    def body(r, carry):
        rs = pl.ds(pl.multiple_of(r * ln_rows, ln_rows), ln_rows)
        y = _layer_norm(z_of(rs), g, b)
        o_ref[rs, :] = y
        if o16_ref is not None:
            o16_ref[rs, :] = y.astype(BF16)
        return carry

    lax.fori_loop(0, n_rows // ln_rows, body, 0, unroll=_LN_UNROLL)


def _outproj_ln_kernel(rows, ln_rows, y0_ref, yodd_ref, ynext_ref, w_ref, x_ref, g_ref, b_ref,
                       o_ref, acca_ref, accb_ref):
    t = pl.program_id(0)

    @pl.when(t == 0)
    def _():
        acca_ref[...] = _dot(y0_ref[...], w_ref[...])

    g = g_ref[...]
    b = b_ref[...]

    def ln(acc_ref, row0):
        for r0 in range(0, rows, ln_rows):
            xs = slice(row0 + r0, row0 + r0 + ln_rows)
            z = ALPHA * x_ref[xs, :] + acc_ref[r0:r0 + ln_rows, :]
            o_ref[xs, :] = _layer_norm(z, g, b)

    accb_ref[...] = _dot(yodd_ref[...], w_ref[...])
    ln(acca_ref, 0)
    acca_ref[...] = _dot(ynext_ref[...], w_ref[...])
    ln(accb_ref, rows)


def _outproj_ln(y2d, w_out, x2d, g, b, rows, ln_rows):
    m = x2d.shape[0]
    n_sub = m // rows
    once = pl.Buffered(1)
    return pl.pallas_call(
        functools.partial(_outproj_ln_kernel, rows, ln_rows),
        grid=(n_sub // 2,),
        in_specs=[
            pl.BlockSpec((rows, D_MODEL), lambda t: (0, 0), pipeline_mode=once),
            pl.BlockSpec((rows, D_MODEL), lambda t: (2 * t + 1, 0)),
            pl.BlockSpec((rows, D_MODEL), lambda t: (jnp.minimum(2 * t + 2, n_sub - 1), 0)),
            pl.BlockSpec((D_MODEL, D_MODEL), lambda t: (0, 0), pipeline_mode=once),
            pl.BlockSpec((2 * rows, D_MODEL), lambda t: (t, 0)),
            pl.BlockSpec((1, D_MODEL), lambda t: (0, 0)),
            pl.BlockSpec((1, D_MODEL), lambda t: (0, 0)),
        ],
        out_specs=pl.BlockSpec((2 * rows, D_MODEL), lambda t: (t, 0)),
        out_shape=jax.ShapeDtypeStruct((m, D_MODEL), F32),
        scratch_shapes=[pltpu.VMEM((rows, D_MODEL), F32), pltpu.VMEM((rows, D_MODEL), F32)],
        compiler_params=_params("arbitrary"),
        name="outproj_ln",
    )(y2d, y2d, y2d, w_out.astype(BF16), x2d, g.astype(F32).reshape(1, D_MODEL),
      b.astype(F32).reshape(1, D_MODEL))


_HALO = BF16_SUBLANE_TILE
_CONV_ROWS = 64


_W_BUFS = 4


def _ffn_kernel(rows, ln_rows, seq_tiles, cols, layer, with_bf16, x_ref, xp_ref, wup_hbm,
                wdn_hbm, cw_ref, cb_ref, g_ref, b_ref, o_ref, *rest):
    o16_ref = rest[0] if with_bf16 else None
    xe_ref, a_ref, bb_ref, hm_ref, acc_ref, wa_buf, wb_buf, wd_buf, sem = rest[-9:]
    i = pl.program_id(0)
    j = pl.program_id(1)
    nj = pl.num_programs(1)
    n = i * nj + j
    n_steps = pl.num_programs(0) * nj

    def weight_copies(step):
        slot = step % _W_BUFS
        c0 = pl.multiple_of((step % nj) * cols, cols)
        return (
            pltpu.make_async_copy(wup_hbm.at[layer, :, pl.ds(c0, cols)], wa_buf.at[slot],
                                  sem.at[0, slot]),
            pltpu.make_async_copy(wup_hbm.at[layer, :, pl.ds(D_FF + c0, cols)], wb_buf.at[slot],
                                  sem.at[1, slot]),
            pltpu.make_async_copy(wdn_hbm.at[layer, pl.ds(c0, cols), :], wd_buf.at[slot],
                                  sem.at[2, slot]),
        )

    @pl.when(n == 0)
    def _():
        for ahead in range(_W_BUFS - 1):
            for cp in weight_copies(ahead):
                cp.start()

    @pl.when(n + _W_BUFS - 1 < n_steps)
    def _():
        for cp in weight_copies(n + _W_BUFS - 1):
            cp.start()

    slot = n % _W_BUFS
    wa_ref = wa_buf.at[slot]
    wb_ref = wb_buf.at[slot]
    wd_ref = wd_buf.at[slot]

    @pl.when(j == 0)
    def _():
        keep = (i % seq_tiles) != 0
        xe_ref[0:_HALO, :] = jnp.where(keep, xp_ref[...], 0.0).astype(BF16)
        xe_ref[_HALO:, :] = x_ref[...].astype(BF16)
        acc_ref[...] = jnp.zeros_like(acc_ref)

    for cp in weight_copies(n):
        cp.wait()
    a_ref[...] = _dot(xe_ref[...], wa_ref[...])
    bb_ref[...] = _dot(xe_ref[_HALO:, :], wb_ref[...])
    cw = cw_ref[...]
    cb = cb_ref[...]
    for r0 in range(0, rows, _CONV_ROWS):
        a_0 = a_ref[_HALO + r0:_HALO + r0 + _CONV_ROWS, :]
        a_1 = a_ref[_HALO + r0 - 1:_HALO + r0 - 1 + _CONV_ROWS, :]
        a_2 = a_ref[_HALO + r0 - 2:_HALO + r0 - 2 + _CONV_ROWS, :]
        conv = cw[0:1] * a_2 + cw[1:2] * a_1 + cw[2:3] * a_0 + cb
        hm = _silu(conv) * bb_ref[r0:r0 + _CONV_ROWS, :]
        hm_ref[r0:r0 + _CONV_ROWS, :] = hm.astype(BF16)
    acc_ref[...] += _dot(hm_ref[...], wd_ref[...])

    @pl.when(j == pl.num_programs(1) - 1)
    def _():
        _ln_rows(ln_rows, rows, lambda rs: ALPHA * x_ref[rs, :] + acc_ref[rs, :],
                 g_ref[...], b_ref[...], o_ref, o16_ref)


def _conv_ffn_ln(x2d, seq, layer, w_up, conv_w, conv_b, w_down, g, b, rows, cols, ln_rows,
                 with_bf16):
    m = x2d.shape[0]
    nj = D_FF // cols
    halo_blocks = rows // _HALO
    out_spec = pl.BlockSpec((rows, D_MODEL), lambda i, j: (i, 0))
    out_f32 = jax.ShapeDtypeStruct((m, D_MODEL), F32)
    return pl.pallas_call(
        functools.partial(_ffn_kernel, rows, ln_rows, seq // rows, cols, layer, with_bf16),
        grid=(m // rows, nj),
        in_specs=[
            pl.BlockSpec((rows, D_MODEL), lambda i, j: (i, 0)),
            pl.BlockSpec((_HALO, D_MODEL), lambda i, j: (jnp.maximum(i * halo_blocks - 1, 0), 0)),
            pl.BlockSpec(memory_space=pl.ANY),
            pl.BlockSpec(memory_space=pl.ANY),
            pl.BlockSpec((CONV_WIDTH, cols), lambda i, j: (0, j)),
            pl.BlockSpec((1, cols), lambda i, j: (0, j)),
            pl.BlockSpec((1, D_MODEL), lambda i, j: (0, 0)),
            pl.BlockSpec((1, D_MODEL), lambda i, j: (0, 0)),
        ],
        out_specs=(out_spec, out_spec) if with_bf16 else out_spec,
        out_shape=((out_f32, jax.ShapeDtypeStruct((m, D_MODEL), BF16)) if with_bf16
                   else out_f32),
        scratch_shapes=[
            pltpu.VMEM((_HALO + rows, D_MODEL), BF16),
            pltpu.VMEM((_HALO + rows, cols), F32),
            pltpu.VMEM((rows, cols), F32),
            pltpu.VMEM((rows, cols), BF16),
            pltpu.VMEM((rows, D_MODEL), F32),
            pltpu.VMEM((_W_BUFS, D_MODEL, cols), BF16),
            pltpu.VMEM((_W_BUFS, D_MODEL, cols), BF16),
            pltpu.VMEM((_W_BUFS, cols, D_MODEL), BF16),
            pltpu.SemaphoreType.DMA((3, _W_BUFS)),
        ],
        compiler_params=_params("arbitrary", "arbitrary"),
        name="conv_ffn_ln",
    )(x2d, x2d, w_up, w_down, conv_w.astype(F32), conv_b.astype(F32).reshape(1, D_FF),
      g.astype(F32).reshape(1, D_MODEL), b.astype(F32).reshape(1, D_MODEL))


def _gmlp_kernel(rows, ln_rows, xc_ref, xn_ref, wu_ref, wv_ref, lng_ref, lnb_ref, ws_ref,
                 bias_ref, o_ref, zv_ref, zu_ref, vn_ref, gate_ref, wc_ref):
    i = pl.program_id(0)
    c = SG_CHUNK
    d = SG_DIM

    @pl.when(i == 0)
    def _():
        causal = (lax.broadcasted_iota(jnp.int32, (c, c), 0)
                  >= lax.broadcasted_iota(jnp.int32, (c, c), 1))
        for g in range(SG_GROUPS):
            wc_ref[g] = jnp.where(causal, ws_ref[g], 0.0).astype(BF16)
        zv_ref[...] = _dot(xc_ref[...], wv_ref[...])

    zu_ref[...] = _dot(xc_ref[...], wu_ref[...])
    lng = lng_ref[...]
    lnb = lnb_ref[...]
    for r0 in range(0, rows, ln_rows):
        rs = slice(r0, r0 + ln_rows)
        vn_ref[rs, :] = _layer_norm(_gelu_exact(zv_ref[rs, :]), lng, lnb).astype(BF16)
    for r0 in range(0, rows, c):
        rs = slice(r0, r0 + c)
        for g in range(SG_GROUPS):
            gs = slice(g * d, (g + 1) * d)
            gate_ref[rs, gs] = _dot(wc_ref[g], vn_ref[rs, gs]) + bias_ref[:, gs]
    zv_ref[...] = _dot(xn_ref[...], wv_ref[...])
    for r0 in range(0, rows, ln_rows):
        rs = slice(r0, r0 + ln_rows)
        o_ref[rs, :] = (_gelu_exact(zu_ref[rs, :]) * gate_ref[rs, :]).astype(BF16)


def _gmlp_mixer(x16, w_in, ln_g, ln_b, w_s, b_s, rows, ln_rows):
    m = x16.shape[0]
    n_tiles = m // rows
    width = SG_GROUPS * SG_DIM
    bias = jnp.broadcast_to(b_s.astype(F32).T[:, :, None], (SG_CHUNK, SG_GROUPS, SG_DIM))
    bias = bias.reshape(SG_CHUNK, width)
    w_in = w_in.astype(BF16)
    once = pl.Buffered(1)
    return pl.pallas_call(
        functools.partial(_gmlp_kernel, rows, ln_rows),
        grid=(n_tiles,),
        in_specs=[
            pl.BlockSpec((rows, D_MODEL), lambda i: (i, 0)),
            pl.BlockSpec((rows, D_MODEL), lambda i: (jnp.minimum(i + 1, n_tiles - 1), 0)),
            pl.BlockSpec((D_MODEL, width), lambda i: (0, 0), pipeline_mode=once),
            pl.BlockSpec((D_MODEL, width), lambda i: (0, 1), pipeline_mode=once),
            pl.BlockSpec((1, width), lambda i: (0, 0)),
            pl.BlockSpec((1, width), lambda i: (0, 0)),
            pl.BlockSpec((SG_GROUPS, SG_CHUNK, SG_CHUNK), lambda i: (0, 0, 0), pipeline_mode=once),
            pl.BlockSpec((SG_CHUNK, width), lambda i: (0, 0), pipeline_mode=once),
        ],
        out_specs=pl.BlockSpec((rows, width), lambda i: (i, 0)),
        out_shape=jax.ShapeDtypeStruct((m, width), BF16),
        scratch_shapes=[
            pltpu.VMEM((rows, width), F32),
            pltpu.VMEM((rows, width), F32),
            pltpu.VMEM((rows, width), BF16),
            pltpu.VMEM((rows, width), F32),
            pltpu.VMEM((SG_GROUPS, SG_CHUNK, SG_CHUNK), BF16),
        ],
        compiler_params=_params("arbitrary"),
        name="gmlp_mixer",
    )(x16, x16, w_in, w_in, ln_g.astype(F32).reshape(1, width),
      ln_b.astype(F32).reshape(1, width), w_s.astype(F32), bias)


def kernel(x, lb_logits, hg_w_in, hg_norm_g, hg_w_out, sg_w_in, sg_ln_g, sg_ln_b, sg_w_s,
           sg_b_s, sg_w_out, ffn_w_up, ffn_conv_w, ffn_conv_b, ffn_w_down, ln1_g, ln1_b,
           ln2_g, ln2_b):
    batch, seq, _ = x.shape
    t = _tiles(batch, seq)
    h = x.astype(F32).reshape(batch * seq, D_MODEL)
    h16 = None
    w_up16 = ffn_w_up.astype(BF16)
    w_down16 = ffn_w_down.astype(BF16)
    for layer in range(DEPTH):
        occ = layer // N_MIXERS
        if layer % N_MIXERS == 0:
            mixed = _hgrn2_mixer(h, batch, seq, layer, lb_logits, hg_w_in[occ], hg_norm_g[occ],
                                 t["hg_rows"])
            w_out = hg_w_out[occ]
        else:
            mixed = _gmlp_mixer(h16 if h16 is not None else h.astype(BF16), sg_w_in[occ],
                                sg_ln_g[occ], sg_ln_b[occ], sg_w_s[occ], sg_b_s[occ],
                                t["sg_rows"], t["ln_rows"])
            w_out = sg_w_out[occ]
        h = _outproj_ln(mixed, w_out, h, ln1_g[layer], ln1_b[layer], t["out_rows"], t["ln_rows"])
        next_is_gmlp = layer + 1 < DEPTH and (layer + 1) % N_MIXERS == 1
        res = _conv_ffn_ln(h, seq, layer, w_up16, ffn_conv_w[layer], ffn_conv_b[layer],
                           w_down16, ln2_g[layer], ln2_b[layer],
                           t["ffn_rows"], t["ffn_cols"], t["ln_rows"], next_is_gmlp)
        h, h16 = res if next_is_gmlp else (res, None)
    return h.reshape(batch, seq, D_MODEL).astype(x.dtype)
```
